```python
import jax, jax.numpy as jnp
from jax import lax
import numpy as np

D_MODEL = 1024
BATCH = 32
SEQ = 2048
DEPTH = 4

N_MIXERS = 3
RMS_EPS = 1e-6
LN_EPS = 1e-5
CONV_CH = D_MODEL
CONV_WIDTH = 31
POOL_CH = D_MODEL
POOL_WINDOWS = (2, 4, 8, 16)
POOL_GROUPS = len(POOL_WINDOWS)
POOL_GROUP_CH = POOL_CH // POOL_GROUPS
SGU_CH = D_MODEL
SGU_GROUPS = 4
SGU_GROUP_CH = SGU_CH // SGU_GROUPS
SGU_CHUNK = 128
FFN_HIDDEN = ((8 * D_MODEL // 3 + 255) // 256) * 256
N_EXPERTS = 8
TOP_K = 2
EXPERT_HIDDEN = 7 * D_MODEL // 2
N_CONV_LAYERS = (DEPTH + 2) // 3
N_POOL_LAYERS = (DEPTH + 1) // 3
N_SGU_LAYERS = DEPTH // 3
N_DENSE_LAYERS = (DEPTH + 1) // 2
N_MOE_LAYERS = DEPTH // 2

kernel_name = "hybrid_conv_pool_sgu_moe_trunk"


def rms_norm(x, g):
    xf = x.astype(jnp.float32)
    y = xf * lax.rsqrt(jnp.mean(xf * xf, axis=-1, keepdims=True) + RMS_EPS)
    return (y * g.astype(jnp.float32)).astype(x.dtype)


def layer_norm(x, g, b):
    xf = x.astype(jnp.float32)
    mu = jnp.mean(xf, axis=-1, keepdims=True)
    var = jnp.mean(jnp.square(xf - mu), axis=-1, keepdims=True)
    y = (xf - mu) * lax.rsqrt(var + LN_EPS)
    return (y * g.astype(jnp.float32) + b.astype(jnp.float32)).astype(x.dtype)


def swiglu(h, w_gate, w_up, w_down):
    return (jax.nn.silu(h @ w_gate) * (h @ w_up)) @ w_down


def conv_mixer(h, w_in, b_in, w_dw, b_dw, ln_g, ln_b, w_out, b_out):
    a, gate = jnp.split(h @ w_in + b_in, 2, axis=-1)
    z = a * jax.nn.sigmoid(gate)
    z = lax.conv_general_dilated(
        z, w_dw[:, None, :], window_strides=(1,),
        padding=[(CONV_WIDTH - 1, 0)],
        dimension_numbers=("NWC", "WIO", "NWC"),
        feature_group_count=CONV_CH) + b_dw
    z = jax.nn.silu(layer_norm(z, ln_g, ln_b))
    return z @ w_out + b_out


def pool_mixer(h, w_in, w_grp, b_grp, scale, w_out):
    p = h @ w_in
    B, S, _ = p.shape
    pg = p.reshape(B, S, POOL_GROUPS, POOL_GROUP_CH).astype(jnp.float32)
    cs = jnp.pad(jnp.cumsum(pg, axis=1), ((0, 0), (1, 0), (0, 0), (0, 0)))
    t = jnp.arange(S)
    means = []
    for g, w in enumerate(POOL_WINDOWS):
        csg = cs[:, :, g]
        start = jnp.maximum(t + 1 - w, 0)
        cnt = jnp.minimum(t + 1, w).astype(jnp.float32)
        means.append((csg[:, 1:] - csg[:, start]) / cnt[None, :, None])
    z = (jnp.stack(means, axis=2) - pg).astype(h.dtype)
    z = jnp.einsum("bsgc,gcd->bsgd", z, w_grp) + b_grp
    z = z.reshape(B, S, POOL_CH) * scale
    return z @ w_out


def sgu_mixer(h, w_in, b_in, ln_g, ln_b, w_s, b_s, w_out):
    z = jax.nn.gelu(h @ w_in + b_in, approximate=False)
    u, v = jnp.split(z, 2, axis=-1)
    v = layer_norm(v, ln_g, ln_b)
    B, S, _ = v.shape
    vc = v.reshape(B, S // SGU_CHUNK, SGU_CHUNK, SGU_GROUPS, SGU_GROUP_CH)
    mask = jnp.tril(jnp.ones((SGU_CHUNK, SGU_CHUNK), dtype=bool))
    ws = jnp.where(mask, w_s, 0)
    mixed = jnp.einsum("gij,bnjgc->bnigc", ws, vc) + b_s.T[None, None, :, :, None]
    return (u * mixed.reshape(B, S, SGU_CH)) @ w_out


def moe_ffn(h, w_router, b_router, w_gate, w_up, w_down):
    logits = (h @ w_router + b_router).astype(jnp.float32)
    top_val, top_idx = lax.top_k(logits, TOP_K)
    gates = jax.nn.softmax(top_val, axis=-1)
    combine = jnp.einsum("bsk,bske->bse", gates,
                         jax.nn.one_hot(top_idx, N_EXPERTS, dtype=jnp.float32)).astype(h.dtype)
    y = jnp.zeros_like(h)
    for e in range(N_EXPERTS):
        y = y + combine[..., e:e + 1] * swiglu(h, w_gate[e], w_up[e], w_down[e])
    return y


def _normal(k, shape, scale):
    return jax.random.normal(k, shape, jnp.float32) * scale


def setup_inputs(seed: int = 0) -> dict:
    key = jax.random.key(seed)
    ks = iter(jax.random.split(key, 40))
    D = D_MODEL
    NA, NB, NC, ND, NM = N_CONV_LAYERS, N_POOL_LAYERS, N_SGU_LAYERS, N_DENSE_LAYERS, N_MOE_LAYERS
    return {
        "x": _normal(next(ks), (BATCH, SEQ, D), 1.0),
        "norm_mix_g": 1.0 + _normal(next(ks), (DEPTH, D), 0.02),
        "norm_ffn_g": 1.0 + _normal(next(ks), (DEPTH, D), 0.02),
        "norm_final_g": 1.0 + _normal(next(ks), (D,), 0.02),
        "conv_w_in": _normal(next(ks), (NA, D, 2 * CONV_CH), D ** -0.5),
        "conv_b_in": _normal(next(ks), (NA, 2 * CONV_CH), 0.02),
        "conv_w_dw": _normal(next(ks), (NA, CONV_WIDTH, CONV_CH), CONV_WIDTH ** -0.5),
        "conv_b_dw": _normal(next(ks), (NA, CONV_CH), 0.02),
        "conv_ln_g": 1.0 + _normal(next(ks), (NA, CONV_CH), 0.02),
        "conv_ln_b": _normal(next(ks), (NA, CONV_CH), 0.02),
        "conv_w_out": _normal(next(ks), (NA, CONV_CH, D), CONV_CH ** -0.5),
        "conv_b_out": _normal(next(ks), (NA, D), 0.02),
        "pool_w_in": _normal(next(ks), (NB, D, POOL_CH), D ** -0.5),
        "pool_w_grp": _normal(next(ks), (NB, POOL_GROUPS, POOL_GROUP_CH, POOL_GROUP_CH), POOL_GROUP_CH ** -0.5),
        "pool_b_grp": _normal(next(ks), (NB, POOL_GROUPS, POOL_GROUP_CH), 0.02),
        "pool_scale": 1.0 + _normal(next(ks), (NB, POOL_CH), 0.02),
        "pool_w_out": _normal(next(ks), (NB, POOL_CH, D), POOL_CH ** -0.5),
        "sgu_w_in": _normal(next(ks), (NC, D, 2 * SGU_CH), D ** -0.5),
        "sgu_b_in": _normal(next(ks), (NC, 2 * SGU_CH), 0.02),
        "sgu_ln_g": 1.0 + _normal(next(ks), (NC, SGU_CH), 0.02),
        "sgu_ln_b": _normal(next(ks), (NC, SGU_CH), 0.02),
        "sgu_w_s": _normal(next(ks), (NC, SGU_GROUPS, SGU_CHUNK, SGU_CHUNK), SGU_CHUNK ** -0.5),
        "sgu_b_s": 1.0 + _normal(next(ks), (NC, SGU_GROUPS, SGU_CHUNK), 0.02),
        "sgu_w_out": _normal(next(ks), (NC, SGU_CH, D), SGU_CH ** -0.5),
        "ffn_w_gate": _normal(next(ks), (ND, D, FFN_HIDDEN), D ** -0.5),
        "ffn_w_up": _normal(next(ks), (ND, D, FFN_HIDDEN), D ** -0.5),
        "ffn_w_down": _normal(next(ks), (ND, FFN_HIDDEN, D), FFN_HIDDEN ** -0.5),
        "moe_w_router": _normal(next(ks), (NM, D, N_EXPERTS), D ** -0.5),
        "moe_b_router": _normal(next(ks), (NM, N_EXPERTS), 0.01),
        "moe_w_gate": _normal(next(ks), (NM, N_EXPERTS, D, EXPERT_HIDDEN), D ** -0.5),
        "moe_w_up": _normal(next(ks), (NM, N_EXPERTS, D, EXPERT_HIDDEN), D ** -0.5),
        "moe_w_down": _normal(next(ks), (NM, N_EXPERTS, EXPERT_HIDDEN, D), EXPERT_HIDDEN ** -0.5),
    }


def reference(x, norm_mix_g, norm_ffn_g, norm_final_g,
              conv_w_in, conv_b_in, conv_w_dw, conv_b_dw, conv_ln_g, conv_ln_b, conv_w_out, conv_b_out,
              pool_w_in, pool_w_grp, pool_b_grp, pool_scale, pool_w_out,
              sgu_w_in, sgu_b_in, sgu_ln_g, sgu_ln_b, sgu_w_s, sgu_b_s, sgu_w_out,
              ffn_w_gate, ffn_w_up, ffn_w_down,
              moe_w_router, moe_b_router, moe_w_gate, moe_w_up, moe_w_down):
    h = x
    for i in range(DEPTH):
        n = rms_norm(h, norm_mix_g[i])
        m, j = i % N_MIXERS, i // N_MIXERS
        if m == 0:
            mix = conv_mixer(n, conv_w_in[j], conv_b_in[j], conv_w_dw[j], conv_b_dw[j],
                             conv_ln_g[j], conv_ln_b[j], conv_w_out[j], conv_b_out[j])
        elif m == 1:
            mix = pool_mixer(n, pool_w_in[j], pool_w_grp[j], pool_b_grp[j], pool_scale[j], pool_w_out[j])
        else:
            mix = sgu_mixer(n, sgu_w_in[j], sgu_b_in[j], sgu_ln_g[j], sgu_ln_b[j],
                            sgu_w_s[j], sgu_b_s[j], sgu_w_out[j])
        h = h + mix
        n = rms_norm(h, norm_ffn_g[i])
        c = i // 2
        if i % 2 == 0:
            h = h + swiglu(n, ffn_w_gate[c], ffn_w_up[c], ffn_w_down[c])
        else:
            h = h + moe_ffn(n, moe_w_router[c], moe_b_router[c], moe_w_gate[c], moe_w_up[c], moe_w_down[c])
    return rms_norm(h, norm_final_g)
```

```python
import functools

import jax
import jax.numpy as jnp
from jax import lax
from jax.experimental import pallas as pl
from jax.experimental.pallas import tpu as pltpu

F32 = jnp.float32
BF16 = jnp.bfloat16

RMS_EPS = 1e-6
LN_EPS = 1e-5
CONV_WIDTH = 31
POOL_WINDOWS = (2, 4, 8, 16)
SGU_GROUPS = 4
SGU_CHUNK = 128
N_EXPERTS = 8

LANES = 128
SUBLANES = 8
BF16_TILE_ROWS = 16
VMEM_LIMIT_BYTES = 56 * 2**20

MIX_TOKENS = 512
FFN_TOKENS = 512
FFN_HIDDEN_STEPS = 2
MOE_BLOCK = 512
MOE_TILE = 1024
MOE_HIDDEN_STEPS = 4
CONV_HALO = 32
POOL_HALO = 16
CONV_ROWS = 32


def _params(semantics):
    return pltpu.CompilerParams(dimension_semantics=semantics, vmem_limit_bytes=VMEM_LIMIT_BYTES)


def _rms(x, g):
    ms = jnp.mean(x * x, axis=-1, keepdims=True)
    return x * lax.rsqrt(ms + RMS_EPS) * g


def _layer_norm(x, g, b):
    mu = jnp.mean(x, axis=-1, keepdims=True)
    xc = x - mu
    var = jnp.mean(xc * xc, axis=-1, keepdims=True)
    return xc * lax.rsqrt(var + LN_EPS) * g + b


def _dot(a, b):
    return jnp.dot(a, b, preferred_element_type=F32)


def _const_spec(shape):
    nd = len(shape)
    return pl.BlockSpec(shape, lambda *_: (0,) * nd)


def _conv_mixer_kernel(h_ref, g_ref, win_ref, bin_ref, wdw_ref, bdw_ref, lng_ref, lnb_ref,
                       wout_ref, bout_ref, o_ref, zbuf, zshift, acc, *, ts, ch):
    @pl.when(pl.program_id(1) == 0)
    def _():
        zbuf[0:CONV_HALO, :] = jnp.zeros((CONV_HALO, ch), F32)

    x = h_ref[...]
    n = _rms(x, g_ref[...]).astype(BF16)
    y = _dot(n, win_ref[...]) + bin_ref[...]
    zbuf[CONV_HALO:CONV_HALO + ts, :] = y[:, :ch] * jax.nn.sigmoid(y[:, ch:])

    first = CONV_HALO - (CONV_WIDTH - 1)
    acc[...] = jnp.broadcast_to(bdw_ref[...], (ts, ch))
    for r in range(SUBLANES):
        if r == 0:
            src = zbuf
        else:
            zshift[0:ts + 24, :] = zbuf[r:r + ts + 24, :]
            src = zshift
        taps = [(q, SUBLANES * q + r - first) for q in range(CONV_HALO // SUBLANES + 1)
                if first <= SUBLANES * q + r <= CONV_HALO]

        def body(i, carry, src=src, taps=taps):
            base = pl.multiple_of(i * CONV_ROWS, CONV_ROWS)
            a = acc[pl.ds(base, CONV_ROWS), :]
            for q, k in taps:
                rows = pl.ds(pl.multiple_of(base + SUBLANES * q, SUBLANES), CONV_ROWS)
                a = a + wdw_ref[k:k + 1, :] * src[rows, :]
            acc[pl.ds(base, CONV_ROWS), :] = a
            return carry

        lax.fori_loop(0, ts // CONV_ROWS, body, 0)

    zbuf[0:CONV_HALO, :] = zbuf[ts:ts + CONV_HALO, :]
    c = _layer_norm(acc[...], lng_ref[...], lnb_ref[...])
    s = (c * jax.nn.sigmoid(c)).astype(BF16)
    o_ref[...] = _dot(s, wout_ref[...]) + bout_ref[...] + x


def _conv_mixer(h, seq, g, w_in, b_in, w_dw, b_dw, ln_g, ln_b, w_out, b_out):
    tokens, d = h.shape
    ch = w_out.shape[0]
    ts = min(MIX_TOKENS, seq)
    steps = seq // ts
    w_dw = jnp.pad(w_dw, ((0, CONV_HALO - CONV_WIDTH), (0, 0)))
    row = lambda v: v.reshape(1, -1)
    tile = pl.BlockSpec((ts, d), lambda s, j: (s * steps + j, 0))
    return pl.pallas_call(
        functools.partial(_conv_mixer_kernel, ts=ts, ch=ch),
        grid=(tokens // seq, steps),
        in_specs=[tile, _const_spec((1, d)), _const_spec((d, 2 * ch)), _const_spec((1, 2 * ch)),
                  _const_spec((CONV_HALO, ch)), _const_spec((1, ch)), _const_spec((1, ch)),
                  _const_spec((1, ch)), _const_spec((ch, d)), _const_spec((1, d))],
        out_specs=tile,
        out_shape=jax.ShapeDtypeStruct((tokens, d), F32),
        scratch_shapes=[pltpu.VMEM((ts + CONV_HALO, ch), F32), pltpu.VMEM((ts + CONV_HALO, ch), F32),
                        pltpu.VMEM((ts, ch), F32)],
        compiler_params=_params(("arbitrary", "arbitrary")),
        name="conv_mixer",
    )(h, row(g), w_in.astype(BF16), row(b_in), w_dw, row(b_dw), row(ln_g), row(ln_b),
      w_out.astype(BF16), row(b_out))


def _pool_mixer_kernel(h_ref, g_ref, win_ref, wgrp_ref, bgrp_ref, scale_ref, wout_ref, o_ref, pbuf,
                       *, ts, ch):
    j = pl.program_id(1)

    @pl.when(j == 0)
    def _():
        pbuf[0:POOL_HALO, :] = jnp.zeros((POOL_HALO, ch), F32)

    x = h_ref[...]
    n = _rms(x, g_ref[...]).astype(BF16)
    p = _dot(n, win_ref[...])
    pbuf[POOL_HALO:POOL_HALO + ts, :] = p
    t = j * ts + lax.broadcasted_iota(jnp.int32, (ts, 1), 0)
    cg = ch // len(POOL_WINDOWS)
    outs = []
    for g, w in enumerate(POOL_WINDOWS):
        cols = slice(g * cg, (g + 1) * cg)
        pg = p[:, cols]
        s = pg
        for back in range(1, w):
            s = s + pbuf[POOL_HALO - back:POOL_HALO - back + ts, cols]
        cnt = jnp.minimum(t + 1, w).astype(F32)
        zg = (s / cnt - pg).astype(BF16)
        outs.append(_dot(zg, wgrp_ref[g]) + bgrp_ref[g:g + 1, :])
    z = (jnp.concatenate(outs, axis=1) * scale_ref[...]).astype(BF16)
    pbuf[0:POOL_HALO, :] = pbuf[ts:ts + POOL_HALO, :]
    o_ref[...] = _dot(z, wout_ref[...]) + x


def _pool_mixer(h, seq, g, w_in, w_grp, b_grp, scale, w_out):
    tokens, d = h.shape
    ch = w_in.shape[1]
    ts = min(MIX_TOKENS, seq)
    steps = seq // ts
    ng, cg, _ = w_grp.shape
    row = lambda v: v.reshape(1, -1)
    tile = pl.BlockSpec((ts, d), lambda s, j: (s * steps + j, 0))
    return pl.pallas_call(
        functools.partial(_pool_mixer_kernel, ts=ts, ch=ch),
        grid=(tokens // seq, steps),
        in_specs=[tile, _const_spec((1, d)), _const_spec((d, ch)), _const_spec((ng, cg, cg)),
                  _const_spec((ng, cg)), _const_spec((1, ch)), _const_spec((ch, d))],
        out_specs=tile,
        out_shape=jax.ShapeDtypeStruct((tokens, d), F32),
        scratch_shapes=[pltpu.VMEM((ts + POOL_HALO, ch), F32)],
        compiler_params=_params(("arbitrary", "arbitrary")),
        name="pool_mixer",
    )(h, row(g), w_in.astype(BF16), w_grp.astype(BF16), b_grp, row(scale), w_out.astype(BF16))


def _sgu_mixer_kernel(h_ref, g_ref, win_ref, bin_ref, lng_ref, lnb_ref, ws_ref, bs_ref, wout_ref,
                      o_ref, *, ts, ch):
    x = h_ref[...]
    n = _rms(x, g_ref[...]).astype(BF16)
    y = _dot(n, win_ref[...]) + bin_ref[...]
    z = 0.5 * y * (1.0 + lax.erf(y * (2.0 ** -0.5)))
    u = z[:, :ch]
    v = _layer_norm(z[:, ch:], lng_ref[...], lnb_ref[...]).astype(BF16)
    cg = ch // SGU_GROUPS
    causal = (lax.broadcasted_iota(jnp.int32, (SGU_CHUNK, SGU_CHUNK), 0)
              >= lax.broadcasted_iota(jnp.int32, (SGU_CHUNK, SGU_CHUNK), 1))
    groups = []
    for g in range(SGU_GROUPS):
        ws = jnp.where(causal, ws_ref[g], 0.0).astype(BF16)
        chunks = [_dot(ws, v[c * SGU_CHUNK:(c + 1) * SGU_CHUNK, g * cg:(g + 1) * cg]) + bs_ref[g]
                  for c in range(ts // SGU_CHUNK)]
        groups.append(jnp.concatenate(chunks, axis=0))
    mixed = jnp.concatenate(groups, axis=1)
    o_ref[...] = _dot((u * mixed).astype(BF16), wout_ref[...]) + x


def _sgu_mixer(h, seq, g, w_in, b_in, ln_g, ln_b, w_s, b_s, w_out):
    tokens, d = h.shape
    ch = w_out.shape[0]
    ts = min(MIX_TOKENS, seq)
    row = lambda v: v.reshape(1, -1)
    tile = pl.BlockSpec((ts, d), lambda i: (i, 0))
    return pl.pallas_call(
        functools.partial(_sgu_mixer_kernel, ts=ts, ch=ch),
        grid=(tokens // ts,),
        in_specs=[tile, _const_spec((1, d)), _const_spec((d, 2 * ch)), _const_spec((1, 2 * ch)),
                  _const_spec((1, ch)), _const_spec((1, ch)),
                  _const_spec((SGU_GROUPS, SGU_CHUNK, SGU_CHUNK)), _const_spec((SGU_GROUPS, SGU_CHUNK, 1)),
                  _const_spec((ch, d))],
        out_specs=tile,
        out_shape=jax.ShapeDtypeStruct((tokens, d), F32),
        compiler_params=_params(("arbitrary",)),
        name="sgu_mixer",
    )(h, row(g), w_in.astype(BF16), row(b_in), row(ln_g), row(ln_b), w_s, b_s[:, :, None],
      w_out.astype(BF16))


def _ffn_kernel(h_ref, g_ref, wg_ref, wu_ref, wd_ref, o_ref, nbuf):
    @pl.when(pl.program_id(1) == 0)
    def _():
        x = h_ref[...]
        nbuf[...] = _rms(x, g_ref[...]).astype(BF16)
        o_ref[...] = x

    n = nbuf[...]
    a = _dot(n, wg_ref[...])
    hidden = (a * jax.nn.sigmoid(a) * _dot(n, wu_ref[...])).astype(BF16)
    o_ref[...] += _dot(hidden, wd_ref[...])


def _dense_ffn(h, g, w_gate, w_up, w_down):
    tokens, d = h.shape
    hidden = w_gate.shape[1]
    tm = min(FFN_TOKENS, tokens)
    th = hidden // FFN_HIDDEN_STEPS
    tile = pl.BlockSpec((tm, d), lambda i, j: (i, 0))
    return pl.pallas_call(
        _ffn_kernel,
        grid=(tokens // tm, FFN_HIDDEN_STEPS),
        in_specs=[tile, _const_spec((1, d)),
                  pl.BlockSpec((d, th), lambda i, j: (0, j)),
                  pl.BlockSpec((d, th), lambda i, j: (0, j)),
                  pl.BlockSpec((th, d), lambda i, j: (j, 0))],
        out_specs=tile,
        out_shape=jax.ShapeDtypeStruct((tokens, d), F32),
        scratch_shapes=[pltpu.VMEM((tm, d), BF16)],
        compiler_params=_params(("arbitrary", "arbitrary")),
        name="dense_ffn",
    )(h, g.reshape(1, -1), w_gate.astype(BF16), w_up.astype(BF16), w_down.astype(BF16))


def _router_kernel(h_ref, g_ref, wr_ref, br_ref, n_ref, meta_ref, meta_t_ref, cnt_ref):
    blk = h_ref.shape[0]
    n = _rms(h_ref[...], g_ref[...])
    n_ref[...] = n.astype(BF16)
    logits = jnp.dot(n, wr_ref[...], preferred_element_type=F32,
                     precision=lax.Precision.HIGHEST) + br_ref[...]
    lane = lax.broadcasted_iota(jnp.int32, (blk, LANES), 1)
    neg = jnp.float32(-1e30)
    lg = jnp.where(lane < N_EXPERTS, logits, neg)
    m1 = jnp.max(lg, axis=1, keepdims=True)
    i1 = jnp.min(jnp.where(lg == m1, lane, LANES), axis=1, keepdims=True)
    lg2 = jnp.where(lane == i1, neg, lg)
    m2 = jnp.max(lg2, axis=1, keepdims=True)
    i2 = jnp.min(jnp.where(lg2 == m2, lane, LANES), axis=1, keepdims=True)
    e = jnp.exp(m2 - m1)
    g1 = 1.0 / (1.0 + e)
    g2 = e / (1.0 + e)
    meta = jnp.where(lane == 0, i1.astype(F32),
                     jnp.where(lane == 1, i2.astype(F32),
                               jnp.where(lane == 2, g1, jnp.where(lane == 3, g2, 0.0))))
    meta_ref[...] = meta
    meta_t_ref[...] = meta.T[0:SUBLANES, :]
    onehot = ((lane == i1) | (lane == i2)).astype(F32)
    cnt_ref[0] = jnp.broadcast_to(jnp.sum(onehot, axis=0, keepdims=True), (SUBLANES, LANES))


def _router(h, g, w_router, b_router):
    tokens, d = h.shape
    blk = min(MOE_BLOCK, tokens)
    nb = tokens // blk
    wr = jnp.pad(w_router, ((0, 0), (0, LANES - N_EXPERTS)))
    br = jnp.pad(b_router, (0, LANES - N_EXPERTS)).reshape(1, LANES)
    return pl.pallas_call(
        _router_kernel,
        grid=(nb,),
        in_specs=[pl.BlockSpec((blk, d), lambda b: (b, 0)), _const_spec((1, d)),
                  _const_spec((d, LANES)), _const_spec((1, LANES))],
        out_specs=[pl.BlockSpec((blk, d), lambda b: (b, 0)),
                   pl.BlockSpec((blk, LANES), lambda b: (b, 0)),
                   pl.BlockSpec((SUBLANES, blk), lambda b: (0, b)),
                   pl.BlockSpec((1, SUBLANES, LANES), lambda b: (b, 0, 0))],
        out_shape=[jax.ShapeDtypeStruct((tokens, d), BF16),
                   jax.ShapeDtypeStruct((tokens, LANES), F32),
                   jax.ShapeDtypeStruct((SUBLANES, tokens), F32),
                   jax.ShapeDtypeStruct((nb, SUBLANES, LANES), F32)],
        compiler_params=_params(("arbitrary",)),
        name="moe_router",
    )(h, g.reshape(1, -1), wr, br)


def _route_plan(counts, tile, n_tiles):
    c = counts[:, 0, :N_EXPERTS].astype(jnp.int32)
    cpad = (c + BF16_TILE_ROWS - 1) // BF16_TILE_ROWS * BF16_TILE_ROWS
    loff = jnp.cumsum(cpad, axis=1) - cpad
    region = (jnp.sum(cpad, axis=0) + tile - 1) // tile * tile
    ends = jnp.cumsum(region)
    gstart = (ends - region)[None, :] + jnp.cumsum(cpad, axis=0) - cpad
    n_valid = ends[-1] // tile
    tidx = jnp.minimum(jnp.arange(n_tiles, dtype=jnp.int32), n_valid - 1)
    tile_expert = jnp.minimum(jnp.searchsorted(ends, tidx * tile, side="right"), N_EXPERTS - 1)
    return dict(
        granules=(cpad // BF16_TILE_ROWS).reshape(-1),
        loff=loff.reshape(-1),
        gstart=gstart.reshape(-1),
        loff_f=loff.astype(F32),
        tile_block=tidx.astype(jnp.int32),
        tile_expert=tile_expert.astype(jnp.int32),
        n_valid=n_valid.reshape(1).astype(jnp.int32),
    )


def _granule_copies(block, granules_ref, loff_ref, gstart_ref, local_ref, global_ref, sem, *,
                    to_global, wait):
    for e in range(N_EXPERTS):
        idx = block * N_EXPERTS + e
        lbase = loff_ref[idx]
        gbase = gstart_ref[idx]

        def body(i, carry, lbase=lbase, gbase=gbase):
            lrows = pl.ds(pl.multiple_of(lbase + i * BF16_TILE_ROWS, BF16_TILE_ROWS), BF16_TILE_ROWS)
            grows = pl.ds(pl.multiple_of(gbase + i * BF16_TILE_ROWS, BF16_TILE_ROWS), BF16_TILE_ROWS)
            if to_global:
                cp = pltpu.make_async_copy(local_ref.at[lrows, :], global_ref.at[grows, :], sem)
            else:
                cp = pltpu.make_async_copy(global_ref.at[grows, :], local_ref.at[lrows, :], sem)
            if wait:
                cp.wait()
            else:
                cp.start()
            return carry

        lax.fori_loop(0, granules_ref[idx], body, 0)


def _gather_kernel(granules_ref, loff_ref, gstart_ref, n_ref, meta_t_ref, loffcol_ref, zeros_ref,
                   sorted_ref, xs, sem, *, blk, rows):
    del zeros_ref
    b = pl.program_id(0)
    e1 = meta_t_ref[0:1, :]
    e2 = meta_t_ref[1:2, :]
    eid = lax.broadcasted_iota(jnp.int32, (SUBLANES, blk), 0).astype(F32)
    ind = ((eid == e1) | (eid == e2)).astype(BF16)
    before = (lax.broadcasted_iota(jnp.int32, (blk, blk), 0)
              < lax.broadcasted_iota(jnp.int32, (blk, blk), 1)).astype(BF16)
    pos = loffcol_ref[0][:, 0:1] + _dot(ind, before)
    p1 = jnp.sum(jnp.where(eid == e1, pos, 0.0), axis=0, keepdims=True).astype(jnp.int32)
    p2 = jnp.sum(jnp.where(eid == e2, pos, 0.0), axis=0, keepdims=True).astype(jnp.int32)
    rid = lax.broadcasted_iota(jnp.int32, (rows, blk), 0)
    perm = ((rid == p1) | (rid == p2)).astype(BF16)
    xs[...] = _dot(perm, n_ref[...]).astype(BF16)
    copies = functools.partial(_granule_copies, b, granules_ref, loff_ref, gstart_ref, xs, sorted_ref,
                               sem.at[0], to_global=True)
    copies(wait=False)
    copies(wait=True)


def _gather(n_bf16, meta_t, plan, blk, rows, total_rows):
    tokens, d = n_bf16.shape
    nb = tokens // blk
    loffcol = jnp.broadcast_to(plan["loff_f"][:, :, None], (nb, N_EXPERTS, LANES))
    zeros = jnp.zeros((total_rows, d), BF16)
    grid_spec = pltpu.PrefetchScalarGridSpec(
        num_scalar_prefetch=3,
        grid=(nb,),
        in_specs=[pl.BlockSpec((blk, d), lambda b, *_: (b, 0)),
                  pl.BlockSpec((SUBLANES, blk), lambda b, *_: (0, b)),
                  pl.BlockSpec((1, N_EXPERTS, LANES), lambda b, *_: (b, 0, 0)),
                  pl.BlockSpec(memory_space=pl.ANY)],
        out_specs=pl.BlockSpec(memory_space=pl.ANY),
        scratch_shapes=[pltpu.VMEM((rows, d), BF16), pltpu.SemaphoreType.DMA((1,))],
    )
    return pl.pallas_call(
        functools.partial(_gather_kernel, blk=blk, rows=rows),
        grid_spec=grid_spec,
        out_shape=jax.ShapeDtypeStruct((total_rows, d), BF16),
        input_output_aliases={6: 0},
        compiler_params=_params(("arbitrary",)),
        name="moe_gather",
    )(plan["granules"], plan["loff"], plan["gstart"], n_bf16, meta_t, loffcol, zeros)


def _expert_kernel(tile_block_ref, tile_expert_ref, n_valid_ref, x_ref, wg_ref, wu_ref, wd_ref, o_ref,
                   acc, *, steps):
    del tile_block_ref, tile_expert_ref
    j = pl.program_id(1)
    used = pl.program_id(0) < n_valid_ref[0]

    @pl.when(jnp.logical_not(used) & (j == 0))
    def _():
        o_ref[...] = jnp.zeros(o_ref.shape, BF16)

    @pl.when(used)
    def _():
        x = x_ref[...]
        a = _dot(x, wg_ref[...])
        hidden = (a * jax.nn.sigmoid(a) * _dot(x, wu_ref[...])).astype(BF16)
        down = _dot(hidden, wd_ref[...])

        @pl.when(j == 0)
        def _():
            acc[...] = down

        @pl.when(j > 0)
        def _():
            acc[...] += down

        @pl.when(j == steps - 1)
        def _():
            o_ref[...] = acc[...].astype(BF16)


def _experts(xs, plan, w_gate, w_up, w_down, tile):
    total_rows, d = xs.shape
    hidden = w_gate.shape[2]
    steps = MOE_HIDDEN_STEPS
    th = hidden // steps

    def hid(i, j, nv):
        return jnp.where(i < nv[0], j, steps - 1)

    grid_spec = pltpu.PrefetchScalarGridSpec(
        num_scalar_prefetch=3,
        grid=(total_rows // tile, steps),
        in_specs=[pl.BlockSpec((tile, d), lambda i, j, tb, te, nv: (tb[i], 0)),
                  pl.BlockSpec((None, d, th), lambda i, j, tb, te, nv: (te[i], 0, hid(i, j, nv))),
                  pl.BlockSpec((None, d, th), lambda i, j, tb, te, nv: (te[i], 0, hid(i, j, nv))),
                  pl.BlockSpec((None, th, d), lambda i, j, tb, te, nv: (te[i], hid(i, j, nv), 0))],
        out_specs=pl.BlockSpec((tile, d), lambda i, j, tb, te, nv: (i, 0)),
        scratch_shapes=[pltpu.VMEM((tile, d), F32)],
    )
    return pl.pallas_call(
        functools.partial(_expert_kernel, steps=steps),
        grid_spec=grid_spec,
        out_shape=jax.ShapeDtypeStruct((total_rows, d), BF16),
        compiler_params=_params(("arbitrary", "arbitrary")),
        name="moe_experts",
    )(plan["tile_block"], plan["tile_expert"], plan["n_valid"], xs, w_gate, w_up, w_down)


def _combine_kernel(granules_ref, loff_ref, gstart_ref, h_ref, meta_ref, loffrow_ref, fg_ref, ys_ref,
                    o_ref, buf, sem, *, blk, rows, final_norm):
    b = pl.program_id(0)

    @pl.when(b == 0)
    def _():
        buf[...] = jnp.zeros(buf.shape, BF16)

    copies = functools.partial(_granule_copies, b, granules_ref, loff_ref, gstart_ref, buf, ys_ref,
                               sem.at[0], to_global=False)
    copies(wait=False)
    meta = meta_ref[...]
    e1, e2, g1, g2 = meta[:, 0:1], meta[:, 1:2], meta[:, 2:3], meta[:, 3:4]
    lane = lax.broadcasted_iota(jnp.int32, (blk, LANES), 1).astype(F32)
    ind = ((lane == e1) | (lane == e2)).astype(BF16)
    before = (lax.broadcasted_iota(jnp.int32, (blk, blk), 1)
              < lax.broadcasted_iota(jnp.int32, (blk, blk), 0)).astype(BF16)
    pos = loffrow_ref[0][0:1, :] + _dot(before, ind)
    p1 = jnp.sum(jnp.where(lane == e1, pos, 0.0), axis=1, keepdims=True).astype(jnp.int32)
    p2 = jnp.sum(jnp.where(lane == e2, pos, 0.0), axis=1, keepdims=True).astype(jnp.int32)
    rid = lax.broadcasted_iota(jnp.int32, (blk, rows), 1)
    q1 = (rid == p1).astype(BF16)
    q2 = (rid == p2).astype(BF16)
    copies(wait=True)
    ys = buf[...]
    out = h_ref[...] + g1 * _dot(q1, ys) + g2 * _dot(q2, ys)
    if final_norm:
        out = _rms(out, fg_ref[...])
    o_ref[...] = out


def _combine(h, meta, ys, plan, blk, rows, final_g, final_norm):
    tokens, d = h.shape
    nb = tokens // blk
    loffrow = jnp.pad(plan["loff_f"], ((0, 0), (0, LANES - N_EXPERTS)))
    loffrow = jnp.broadcast_to(loffrow[:, None, :], (nb, SUBLANES, LANES))
    grid_spec = pltpu.PrefetchScalarGridSpec(
        num_scalar_prefetch=3,
        grid=(nb,),
        in_specs=[pl.BlockSpec((blk, d), lambda b, *_: (b, 0)),
                  pl.BlockSpec((blk, LANES), lambda b, *_: (b, 0)),
                  pl.BlockSpec((1, SUBLANES, LANES), lambda b, *_: (b, 0, 0)),
                  pl.BlockSpec((1, d), lambda b, *_: (0, 0)),
                  pl.BlockSpec(memory_space=pl.ANY)],
        out_specs=pl.BlockSpec((blk, d), lambda b, *_: (b, 0)),
        scratch_shapes=[pltpu.VMEM((rows, d), BF16), pltpu.SemaphoreType.DMA((1,))],
    )
    return pl.pallas_call(
        functools.partial(_combine_kernel, blk=blk, rows=rows, final_norm=final_norm),
        grid_spec=grid_spec,
        out_shape=jax.ShapeDtypeStruct((tokens, d), F32),
        compiler_params=_params(("arbitrary",)),
        name="moe_combine",
    )(plan["granules"], plan["loff"], plan["gstart"], h, meta, loffrow, final_g.reshape(1, -1), ys)


def _moe_ffn(h, g, w_router, b_router, w_gate, w_up, w_down, final_g, final_norm):
    tokens, _ = h.shape
    blk = min(MOE_BLOCK, tokens)
    nb = tokens // blk
    tile = MOE_TILE
    rows = 2 * blk + N_EXPERTS * BF16_TILE_ROWS
    max_rows = 2 * tokens + nb * N_EXPERTS * (BF16_TILE_ROWS - 1) + N_EXPERTS * (tile - 1)
    n_tiles = -(-max_rows // tile)
    n_bf16, meta, meta_t, counts = _router(h, g, w_router, b_router)
    plan = _route_plan(counts, tile, n_tiles)
    xs = _gather(n_bf16, meta_t, plan, blk, rows, n_tiles * tile)
    ys = _experts(xs, plan, w_gate.astype(BF16), w_up.astype(BF16), w_down.astype(BF16), tile)
    return _combine(h, meta, ys, plan, blk, rows, final_g, final_norm)


def _final_norm_kernel(h_ref, g_ref, o_ref):
    o_ref[...] = _rms(h_ref[...], g_ref[...])


def _final_norm(h, g):
    tokens, d = h.shape
    tm = min(FFN_TOKENS, tokens)
    tile = pl.BlockSpec((tm, d), lambda i: (i, 0))
    return pl.pallas_call(
        _final_norm_kernel, grid=(tokens // tm,), in_specs=[tile, _const_spec((1, d))], out_specs=tile,
        out_shape=jax.ShapeDtypeStruct((tokens, d), F32), compiler_params=_params(("arbitrary",)),
        name="final_norm",
    )(h, g.reshape(1, -1))


def kernel(x, norm_mix_g, norm_ffn_g, norm_final_g, conv_w_in, conv_b_in, conv_w_dw, conv_b_dw, conv_ln_g, conv_ln_b, conv_w_out, conv_b_out, pool_w_in, pool_w_grp, pool_b_grp, pool_scale, pool_w_out, sgu_w_in, sgu_b_in, sgu_ln_g, sgu_ln_b, sgu_w_s, sgu_b_s, sgu_w_out, ffn_w_gate, ffn_w_up, ffn_w_down, moe_w_router, moe_b_router, moe_w_gate, moe_w_up, moe_w_down):
    batch, seq, d = x.shape
    depth = norm_mix_g.shape[0]
    h = x.reshape(batch * seq, d)
    for i in range(depth):
        m, j = i % 3, i // 3
        if m == 0:
            h = _conv_mixer(h, seq, norm_mix_g[i], conv_w_in[j], conv_b_in[j], conv_w_dw[j], conv_b_dw[j],
                            conv_ln_g[j], conv_ln_b[j], conv_w_out[j], conv_b_out[j])
        elif m == 1:
            h = _pool_mixer(h, seq, norm_mix_g[i], pool_w_in[j], pool_w_grp[j], pool_b_grp[j],
                            pool_scale[j], pool_w_out[j])
        else:
            h = _sgu_mixer(h, seq, norm_mix_g[i], sgu_w_in[j], sgu_b_in[j], sgu_ln_g[j], sgu_ln_b[j],
                           sgu_w_s[j], sgu_b_s[j], sgu_w_out[j])
        c = i // 2
        last = i == depth - 1
        if i % 2 == 0:
            h = _dense_ffn(h, norm_ffn_g[i], ffn_w_gate[c], ffn_w_up[c], ffn_w_down[c])
            if last:
                h = _final_norm(h, norm_final_g)
        else:
            h = _moe_ffn(h, norm_ffn_g[i], moe_w_router[c], moe_b_router[c], moe_w_gate[c], moe_w_up[c],
                         moe_w_down[c], norm_final_g, last)
    return h.reshape(batch, seq, d)
```

```python
import functools

import jax
import jax.numpy as jnp
from jax import lax
from jax.experimental import pallas as pl
from jax.experimental.pallas import tpu as pltpu

F32 = jnp.float32
BF16 = jnp.bfloat16

RMS_EPS = 1e-6
LN_EPS = 1e-5
CONV_WIDTH = 31
POOL_WINDOWS = (2, 4, 8, 16)
SGU_GROUPS = 4
SGU_CHUNK = 128
N_EXPERTS = 8

LANES = 128
SUBLANES = 8
BF16_TILE_ROWS = 16
MXU_WIDTH = 256
VMEM_LIMIT_BYTES = 56 * 2**20

MIX_TOKENS = 512
FFN_TOKENS = 512
MOE_BLOCK = 512
MOE_TILE = 1024
MOE_HIDDEN_STEPS = 2
GATE_PIECES = 3
ZERO_ROWS = 256
CONV_HALO = 32
POOL_HALO = 16
CONV_ROWS = 64


def _params(semantics):
    return pltpu.CompilerParams(dimension_semantics=semantics, vmem_limit_bytes=VMEM_LIMIT_BYTES)


def _rms(x, g):
    ms = jnp.mean(x * x, axis=-1, keepdims=True)
    return x * lax.rsqrt(ms + RMS_EPS) * g


def _layer_norm(x, g, b):
    mu = jnp.mean(x, axis=-1, keepdims=True)
    xc = x - mu
    var = jnp.mean(xc * xc, axis=-1, keepdims=True)
    return xc * lax.rsqrt(var + LN_EPS) * g + b


def _dot(a, b):
    return jnp.dot(a, b, preferred_element_type=F32)


def _const_spec(shape):
    nd = len(shape)
    return pl.BlockSpec(shape, lambda *_: (0,) * nd, pipeline_mode=pl.Buffered(1))


def _conv_mixer_kernel(h_ref, g_ref, win_ref, bin_ref, wdw_ref, bdw_ref, lng_ref, lnb_ref,
                       wout_ref, bout_ref, o_ref, zbuf, acc, *, ts, ch):
    nblk = ch // LANES

    @pl.when(pl.program_id(1) == 0)
    def _():
        zbuf[:, 0:CONV_HALO, :] = jnp.zeros((nblk, CONV_HALO, LANES), F32)

    x = h_ref[...]
    n = _rms(x, g_ref[...]).astype(BF16)
    y = _dot(n, win_ref[...]) + bin_ref[...]
    z = y[:, :ch] * jax.nn.sigmoid(y[:, ch:])
    for jb in range(nblk):
        zbuf[jb, CONV_HALO:CONV_HALO + ts, :] = z[:, jb * LANES:(jb + 1) * LANES]

    first = CONV_HALO - (CONV_WIDTH - 1)
    for jb in range(nblk):
        cols = slice(jb * LANES, (jb + 1) * LANES)

        def body(i, carry, jb=jb, cols=cols):
            base = pl.multiple_of(i * CONV_ROWS, CONV_ROWS)
            a = jnp.broadcast_to(bdw_ref[:, cols], (CONV_ROWS, LANES))
            for k in range(CONV_WIDTH):
                a = a + wdw_ref[k:k + 1, cols] * zbuf[jb, pl.ds(base + first + k, CONV_ROWS), :]
            acc[pl.ds(base, CONV_ROWS), cols] = a
            return carry

        lax.fori_loop(0, ts // CONV_ROWS, body, 0)

    zbuf[:, 0:CONV_HALO, :] = zbuf[:, ts:ts + CONV_HALO, :]
    c = _layer_norm(acc[...], lng_ref[...], lnb_ref[...])
    s = (c * jax.nn.sigmoid(c)).astype(BF16)
    o_ref[...] = _dot(s, wout_ref[...]) + bout_ref[...] + x


def _conv_mixer(h, seq, g, w_in, b_in, w_dw, b_dw, ln_g, ln_b, w_out, b_out):
    tokens, d = h.shape
    ch = w_out.shape[0]
    ts = min(MIX_TOKENS, seq)
    steps = seq // ts
    w_dw = jnp.pad(w_dw, ((0, CONV_HALO - CONV_WIDTH), (0, 0)))
    row = lambda v: v.reshape(1, -1)
    tile = pl.BlockSpec((ts, d), lambda s, j: (s * steps + j, 0))
    return pl.pallas_call(
        functools.partial(_conv_mixer_kernel, ts=ts, ch=ch),
        grid=(tokens // seq, steps),
        in_specs=[tile, _const_spec((1, d)), _const_spec((d, 2 * ch)), _const_spec((1, 2 * ch)),
                  _const_spec((CONV_HALO, ch)), _const_spec((1, ch)), _const_spec((1, ch)),
                  _const_spec((1, ch)), _const_spec((ch, d)), _const_spec((1, d))],
        out_specs=tile,
        out_shape=jax.ShapeDtypeStruct((tokens, d), F32),
        scratch_shapes=[pltpu.VMEM((ch // LANES, ts + CONV_HALO, LANES), F32), pltpu.VMEM((ts, ch), F32)],
        compiler_params=_params(("arbitrary", "arbitrary")),
        name="conv_mixer",
    )(h, row(g), w_in.astype(BF16), row(b_in), w_dw, row(b_dw), row(ln_g), row(ln_b),
      w_out.astype(BF16), row(b_out))


def _pool_mixer_kernel(h_ref, g_ref, win_ref, wgrp_ref, bgrp_ref, scale_ref, wout_ref, o_ref, pbuf,
                       *, ts, ch):
    j = pl.program_id(1)

    @pl.when(j == 0)
    def _():
        pbuf[0:POOL_HALO, :] = jnp.zeros((POOL_HALO, ch), F32)

    x = h_ref[...]
    n = _rms(x, g_ref[...]).astype(BF16)
    p = _dot(n, win_ref[...])
    pbuf[POOL_HALO:POOL_HALO + ts, :] = p
    t = j * ts + lax.broadcasted_iota(jnp.int32, (ts, 1), 0)
    cg = ch // len(POOL_WINDOWS)
    outs = []
    for g, w in enumerate(POOL_WINDOWS):
        cols = slice(g * cg, (g + 1) * cg)
        pg = p[:, cols]
        s = pg
        for back in range(1, w):
            s = s + pbuf[POOL_HALO - back:POOL_HALO - back + ts, cols]
        cnt = jnp.minimum(t + 1, w).astype(F32)
        zg = (s / cnt - pg).astype(BF16)
        outs.append(_dot(zg, wgrp_ref[g]) + bgrp_ref[g:g + 1, :])
    z = (jnp.concatenate(outs, axis=1) * scale_ref[...]).astype(BF16)
    pbuf[0:POOL_HALO, :] = pbuf[ts:ts + POOL_HALO, :]
    o_ref[...] = _dot(z, wout_ref[...]) + x


def _pool_mixer(h, seq, g, w_in, w_grp, b_grp, scale, w_out):
    tokens, d = h.shape
    ch = w_in.shape[1]
    ts = min(MIX_TOKENS, seq)
    steps = seq // ts
    ng, cg, _ = w_grp.shape
    row = lambda v: v.reshape(1, -1)
    tile = pl.BlockSpec((ts, d), lambda s, j: (s * steps + j, 0))
    return pl.pallas_call(
        functools.partial(_pool_mixer_kernel, ts=ts, ch=ch),
        grid=(tokens // seq, steps),
        in_specs=[tile, _const_spec((1, d)), _const_spec((d, ch)), _const_spec((ng, cg, cg)),
                  _const_spec((ng, cg)), _const_spec((1, ch)), _const_spec((ch, d))],
        out_specs=tile,
        out_shape=jax.ShapeDtypeStruct((tokens, d), F32),
        scratch_shapes=[pltpu.VMEM((ts + POOL_HALO, ch), F32)],
        compiler_params=_params(("arbitrary", "arbitrary")),
        name="pool_mixer",
    )(h, row(g), w_in.astype(BF16), w_grp.astype(BF16), b_grp, row(scale), w_out.astype(BF16))


def _sgu_mixer_kernel(h_ref, g_ref, win_ref, bin_ref, lng_ref, lnb_ref, ws_ref, bs_ref, wout_ref,
                      o_ref, *, ts, ch):
    x = h_ref[...]
    n = _rms(x, g_ref[...]).astype(BF16)
    y = _dot(n, win_ref[...]) + bin_ref[...]
    z = 0.5 * y * (1.0 + lax.erf(y * (2.0 ** -0.5)))
    u = z[:, :ch]
    v = _layer_norm(z[:, ch:], lng_ref[...], lnb_ref[...]).astype(BF16)
    cg = ch // SGU_GROUPS
    causal = (lax.broadcasted_iota(jnp.int32, (SGU_CHUNK, SGU_CHUNK), 0)
              >= lax.broadcasted_iota(jnp.int32, (SGU_CHUNK, SGU_CHUNK), 1))
    groups = []
    for g in range(SGU_GROUPS):
        ws = jnp.where(causal, ws_ref[g], 0.0).astype(BF16)
        chunks = [_dot(ws, v[c * SGU_CHUNK:(c + 1) * SGU_CHUNK, g * cg:(g + 1) * cg]) + bs_ref[g]
                  for c in range(ts // SGU_CHUNK)]
        groups.append(jnp.concatenate(chunks, axis=0))
    mixed = jnp.concatenate(groups, axis=1)
    o_ref[...] = _dot((u * mixed).astype(BF16), wout_ref[...]) + x


def _sgu_mixer(h, seq, g, w_in, b_in, ln_g, ln_b, w_s, b_s, w_out):
    tokens, d = h.shape
    ch = w_out.shape[0]
    ts = min(MIX_TOKENS, seq)
    row = lambda v: v.reshape(1, -1)
    tile = pl.BlockSpec((ts, d), lambda i: (i, 0))
    return pl.pallas_call(
        functools.partial(_sgu_mixer_kernel, ts=ts, ch=ch),
        grid=(tokens // ts,),
        in_specs=[tile, _const_spec((1, d)), _const_spec((d, 2 * ch)), _const_spec((1, 2 * ch)),
                  _const_spec((1, ch)), _const_spec((1, ch)),
                  _const_spec((SGU_GROUPS, SGU_CHUNK, SGU_CHUNK)), _const_spec((SGU_GROUPS, SGU_CHUNK, 1)),
                  _const_spec((ch, d))],
        out_specs=tile,
        out_shape=jax.ShapeDtypeStruct((tokens, d), F32),
        compiler_params=_params(("arbitrary",)),
        name="sgu_mixer",
    )(h, row(g), w_in.astype(BF16), row(b_in), row(ln_g), row(ln_b), w_s, b_s[:, :, None],
      w_out.astype(BF16))


def _swiglu_hidden(x, wg_ref, wu_ref, hbuf):
    for c in range(hbuf.shape[1] // MXU_WIDTH):
        cols = slice(c * MXU_WIDTH, (c + 1) * MXU_WIDTH)
        a = _dot(x, wg_ref[:, cols])
        hbuf[:, cols] = (a * jax.nn.sigmoid(a) * _dot(x, wu_ref[:, cols])).astype(BF16)


def _ffn_kernel(h_ref, g_ref, wg_ref, wu_ref, wd_ref, o_ref, nbuf, hbuf):
    x = h_ref[...]
    nbuf[...] = _rms(x, g_ref[...]).astype(BF16)
    _swiglu_hidden(nbuf[...], wg_ref, wu_ref, hbuf)
    o_ref[...] = x + _dot(hbuf[...], wd_ref[...])


def _dense_ffn(h, g, w_gate, w_up, w_down):
    tokens, d = h.shape
    hidden = w_gate.shape[1]
    tm = min(FFN_TOKENS, tokens)
    tile = pl.BlockSpec((tm, d), lambda i: (i, 0))
    return pl.pallas_call(
        _ffn_kernel,
        grid=(tokens // tm,),
        in_specs=[tile, _const_spec((1, d)), _const_spec((d, hidden)), _const_spec((d, hidden)),
                  _const_spec((hidden, d))],
        out_specs=tile,
        out_shape=jax.ShapeDtypeStruct((tokens, d), F32),
        scratch_shapes=[pltpu.VMEM((tm, d), BF16), pltpu.VMEM((tm, hidden), BF16)],
        compiler_params=_params(("arbitrary",)),
        name="dense_ffn",
    )(h, g.reshape(1, -1), w_gate.astype(BF16), w_up.astype(BF16), w_down.astype(BF16))


def _router_kernel(h_ref, g_ref, wr_ref, br_ref, n_ref, meta_ref, meta_t_ref, cnt_ref):
    blk = h_ref.shape[0]
    n = _rms(h_ref[...], g_ref[...])
    n_ref[...] = n.astype(BF16)
    logits = jnp.dot(n, wr_ref[...], preferred_element_type=F32,
                     precision=lax.Precision.HIGHEST) + br_ref[...]
    lane = lax.broadcasted_iota(jnp.int32, (blk, LANES), 1)
    neg = jnp.float32(-1e30)
    lg = jnp.where(lane < N_EXPERTS, logits, neg)
    m1 = jnp.max(lg, axis=1, keepdims=True)
    i1 = jnp.min(jnp.where(lg == m1, lane, LANES), axis=1, keepdims=True)
    lg2 = jnp.where(lane == i1, neg, lg)
    m2 = jnp.max(lg2, axis=1, keepdims=True)
    i2 = jnp.min(jnp.where(lg2 == m2, lane, LANES), axis=1, keepdims=True)
    e = jnp.exp(m2 - m1)
    g1 = 1.0 / (1.0 + e)
    g2 = e / (1.0 + e)
    meta = jnp.where(lane == 0, i1.astype(F32),
                     jnp.where(lane == 1, i2.astype(F32),
                               jnp.where(lane == 2, g1, jnp.where(lane == 3, g2, 0.0))))
    meta_ref[...] = meta
    meta_t_ref[...] = meta.T[0:SUBLANES, :]
    onehot = ((lane == i1) | (lane == i2)).astype(F32)
    cnt_ref[0] = jnp.broadcast_to(jnp.sum(onehot, axis=0, keepdims=True), (SUBLANES, LANES))


def _router(h, g, w_router, b_router):
    tokens, d = h.shape
    blk = min(MOE_BLOCK, tokens)
    nb = tokens // blk
    wr = jnp.pad(w_router, ((0, 0), (0, LANES - N_EXPERTS)))
    br = jnp.pad(b_router, (0, LANES - N_EXPERTS)).reshape(1, LANES)
    return pl.pallas_call(
        _router_kernel,
        grid=(nb,),
        in_specs=[pl.BlockSpec((blk, d), lambda b: (b, 0)), _const_spec((1, d)),
                  _const_spec((d, LANES)), _const_spec((1, LANES))],
        out_specs=[pl.BlockSpec((blk, d), lambda b: (b, 0)),
                   pl.BlockSpec((blk, LANES), lambda b: (b, 0)),
                   pl.BlockSpec((SUBLANES, blk), lambda b: (0, b)),
                   pl.BlockSpec((1, SUBLANES, LANES), lambda b: (b, 0, 0))],
        out_shape=[jax.ShapeDtypeStruct((tokens, d), BF16),
                   jax.ShapeDtypeStruct((tokens, LANES), F32),
                   jax.ShapeDtypeStruct((SUBLANES, tokens), F32),
                   jax.ShapeDtypeStruct((nb, SUBLANES, LANES), F32)],
        compiler_params=_params(("arbitrary",)),
        name="moe_router",
    )(h, g.reshape(1, -1), wr, br)


def _route_plan(counts, tile, n_tiles):
    c = counts[:, 0, :N_EXPERTS].astype(jnp.int32)
    cpad = (c + BF16_TILE_ROWS - 1) // BF16_TILE_ROWS * BF16_TILE_ROWS
    loff = jnp.cumsum(cpad, axis=1) - cpad
    used = jnp.sum(cpad, axis=0)
    region = (used + tile - 1) // tile * tile
    ends = jnp.cumsum(region)
    gstart = (ends - region)[None, :] + jnp.cumsum(cpad, axis=0) - cpad
    n_valid = ends[-1] // tile
    tidx = jnp.minimum(jnp.arange(n_tiles, dtype=jnp.int32), n_valid - 1)
    tile_expert = jnp.minimum(jnp.searchsorted(ends, tidx * tile, side="right"), N_EXPERTS - 1)
    return dict(
        granules=(cpad // BF16_TILE_ROWS).reshape(-1),
        loff=loff.reshape(-1),
        gstart=gstart.reshape(-1),
        tail_start=jnp.concatenate([ends - region + used, ends[-1:]]),
        tail_granules=jnp.concatenate([(region - used) // BF16_TILE_ROWS,
                                       (n_tiles * tile - ends[-1:]) // ZERO_ROWS]),
        loff_f=loff.astype(F32),
        tile_block=tidx.astype(jnp.int32),
        tile_expert=tile_expert.astype(jnp.int32),
        n_valid=n_valid.reshape(1).astype(jnp.int32),
    )


def _rows16(base, i):
    return pl.ds(pl.multiple_of(base + i * BF16_TILE_ROWS, BF16_TILE_ROWS), BF16_TILE_ROWS)


def _granule_loop(count, copy_at, wait):
    def body(i, carry):
        cp = copy_at(i)
        if wait:
            cp.wait()
        else:
            cp.start()
        return carry

    lax.fori_loop(0, count, body, 0)


def _block_copies(block, granules_ref, loff_ref, gstart_ref, local_ref, global_ref, sem, *,
                  to_global, wait):
    for e in range(N_EXPERTS):
        idx = block * N_EXPERTS + e
        lbase = loff_ref[idx]
        gbase = gstart_ref[idx]

        def copy_at(i, lbase=lbase, gbase=gbase):
            local = local_ref.at[_rows16(lbase, i), :]
            remote = global_ref.at[_rows16(gbase, i), :]
            if to_global:
                return pltpu.make_async_copy(local, remote, sem)
            return pltpu.make_async_copy(remote, local, sem)

        _granule_loop(granules_ref[idx], copy_at, wait)


def _tail_fill(tail_start_ref, tail_granules_ref, zrows, sorted_ref, sem, *, wait):
    for e in range(N_EXPERTS):
        base = tail_start_ref[e]

        def copy_at(i, base=base):
            return pltpu.make_async_copy(zrows.at[0:BF16_TILE_ROWS, :], sorted_ref.at[_rows16(base, i), :], sem)

        _granule_loop(tail_granules_ref[e], copy_at, wait)

    chunk = zrows.shape[0]
    end_base = tail_start_ref[N_EXPERTS]

    def end_copy_at(i):
        rows = pl.ds(pl.multiple_of(end_base + i * chunk, chunk), chunk)
        return pltpu.make_async_copy(zrows, sorted_ref.at[rows, :], sem)

    _granule_loop(tail_granules_ref[N_EXPERTS], end_copy_at, wait)


def _gate_pieces(g, lane):
    out = jnp.zeros(lane.shape, F32)
    rest = g
    for i in range(GATE_PIECES):
        piece = rest.astype(BF16).astype(F32)
        out = jnp.where(lane == i, piece, out)
        rest = rest - piece
    return out.astype(BF16)


def _gather_kernel(granules_ref, loff_ref, gstart_ref, tail_start_ref, tail_granules_ref,
                   n_ref, meta_ref, meta_t_ref, loffcol_ref, sorted_ref, xs, zrows, sem, *, blk, rows, d):
    b = pl.program_id(0)
    e1 = meta_t_ref[0:1, :]
    e2 = meta_t_ref[1:2, :]
    eid = lax.broadcasted_iota(jnp.int32, (SUBLANES, blk), 0).astype(F32)
    ind = ((eid == e1) | (eid == e2)).astype(BF16)
    before = (lax.broadcasted_iota(jnp.int32, (blk, blk), 0)
              < lax.broadcasted_iota(jnp.int32, (blk, blk), 1)).astype(BF16)
    pos = loffcol_ref[0][:, 0:1] + _dot(ind, before)
    p1 = jnp.sum(jnp.where(eid == e1, pos, 0.0), axis=0, keepdims=True).astype(jnp.int32)
    p2 = jnp.sum(jnp.where(eid == e2, pos, 0.0), axis=0, keepdims=True).astype(jnp.int32)
    rid = lax.broadcasted_iota(jnp.int32, (rows, blk), 0)
    perm1 = (rid == p1).astype(BF16)
    perm2 = (rid == p2).astype(BF16)
    xs[:, 0:d] = _dot(perm1 + perm2, n_ref[...]).astype(BF16)
    meta = meta_ref[...]
    lane = lax.broadcasted_iota(jnp.int32, (blk, LANES), 1)
    gates = _dot(perm1, _gate_pieces(meta[:, 2:3], lane)) + _dot(perm2, _gate_pieces(meta[:, 3:4], lane))
    xs[:, d:d + LANES] = gates.astype(BF16)
    copies = functools.partial(_block_copies, b, granules_ref, loff_ref, gstart_ref, xs, sorted_ref,
                               sem.at[0], to_global=True)
    copies(wait=False)
    copies(wait=True)

    @pl.when(b == pl.num_programs(0) - 1)
    def _():
        zrows[...] = jnp.zeros(zrows.shape, BF16)
        fill = functools.partial(_tail_fill, tail_start_ref, tail_granules_ref, zrows, sorted_ref, sem.at[0])
        fill(wait=False)
        fill(wait=True)


def _gather(n_bf16, meta, meta_t, plan, blk, rows, total_rows):
    tokens, d = n_bf16.shape
    nb = tokens // blk
    loffcol = jnp.broadcast_to(plan["loff_f"][:, :, None], (nb, N_EXPERTS, LANES))
    width = d + LANES
    grid_spec = pltpu.PrefetchScalarGridSpec(
        num_scalar_prefetch=5,
        grid=(nb,),
        in_specs=[pl.BlockSpec((blk, d), lambda b, *_: (b, 0)),
                  pl.BlockSpec((blk, LANES), lambda b, *_: (b, 0)),
                  pl.BlockSpec((SUBLANES, blk), lambda b, *_: (0, b)),
                  pl.BlockSpec((1, N_EXPERTS, LANES), lambda b, *_: (b, 0, 0))],
        out_specs=pl.BlockSpec(memory_space=pl.ANY),
        scratch_shapes=[pltpu.VMEM((rows, width), BF16), pltpu.VMEM((ZERO_ROWS, width), BF16),
                        pltpu.SemaphoreType.DMA((1,))],
    )
    return pl.pallas_call(
        functools.partial(_gather_kernel, blk=blk, rows=rows, d=d),
        grid_spec=grid_spec,
        out_shape=jax.ShapeDtypeStruct((total_rows, width), BF16),
        compiler_params=_params(("arbitrary",)),
        name="moe_gather",
    )(plan["granules"], plan["loff"], plan["gstart"], plan["tail_start"], plan["tail_granules"],
      n_bf16, meta, meta_t, loffcol)


def _expert_kernel(tile_block_ref, tile_expert_ref, n_valid_ref, x_ref, wg_ref, wu_ref, wd_ref, o_ref,
                   acc, hbuf, *, d):
    del tile_block_ref, tile_expert_ref
    j = pl.program_id(1)
    used = pl.program_id(0) < n_valid_ref[0]

    @pl.when(jnp.logical_not(used) & (j == 0))
    def _():
        o_ref[...] = jnp.zeros(o_ref.shape, BF16)

    @pl.when((pl.program_id(0) == 0) & (j == 0))
    def _():
        acc[...] = jnp.zeros(acc.shape, F32)

    @pl.when(used)
    def _():
        _swiglu_hidden(x_ref[:, 0:d], wg_ref, wu_ref, hbuf)
        down = _dot(hbuf[...], wd_ref[...])
        total = jnp.where(j == 0, down, acc[...] + down)
        acc[...] = total
        pieces = x_ref[:, d:d + LANES].astype(F32)
        gate = sum(pieces[:, i:i + 1] for i in range(1, GATE_PIECES)) + pieces[:, 0:1]
        o_ref[...] = (total * gate).astype(BF16)


def _experts(xs, plan, w_gate, w_up, w_down, tile):
    total_rows = xs.shape[0]
    d = w_gate.shape[1]
    hidden = w_gate.shape[2]
    steps = MOE_HIDDEN_STEPS
    th = hidden // steps

    def hid(i, j, nv):
        return jnp.where(i < nv[0], j, steps - 1)

    grid_spec = pltpu.PrefetchScalarGridSpec(
        num_scalar_prefetch=3,
        grid=(total_rows // tile, steps),
        in_specs=[pl.BlockSpec((tile, d + LANES), lambda i, j, tb, te, nv: (tb[i], 0)),
                  pl.BlockSpec((None, d, th), lambda i, j, tb, te, nv: (te[i], 0, hid(i, j, nv))),
                  pl.BlockSpec((None, d, th), lambda i, j, tb, te, nv: (te[i], 0, hid(i, j, nv))),
                  pl.BlockSpec((None, th, d), lambda i, j, tb, te, nv: (te[i], hid(i, j, nv), 0))],
        out_specs=pl.BlockSpec((tile, d), lambda i, j, tb, te, nv: (i, 0)),
        scratch_shapes=[pltpu.VMEM((tile, d), F32), pltpu.VMEM((tile, th), BF16)],
    )
    return pl.pallas_call(
        functools.partial(_expert_kernel, d=d),
        grid_spec=grid_spec,
        out_shape=jax.ShapeDtypeStruct((total_rows, d), BF16),
        compiler_params=_params(("arbitrary", "arbitrary")),
        name="moe_experts",
    )(plan["tile_block"], plan["tile_expert"], plan["n_valid"], xs, w_gate, w_up, w_down)


def _combine_kernel(granules_ref, loff_ref, gstart_ref, h_ref, meta_ref, loffrow_ref, fg_ref, ys_ref,
                    o_ref, buf, sem, *, blk, rows, final_norm):
    b = pl.program_id(0)

    @pl.when(b == 0)
    def _():
        buf[...] = jnp.zeros(buf.shape, BF16)

    copies = functools.partial(_block_copies, b, granules_ref, loff_ref, gstart_ref, buf, ys_ref,
                               sem.at[0], to_global=False)
    copies(wait=False)
    meta = meta_ref[...]
    e1, e2 = meta[:, 0:1], meta[:, 1:2]
    lane = lax.broadcasted_iota(jnp.int32, (blk, LANES), 1).astype(F32)
    ind = ((lane == e1) | (lane == e2)).astype(BF16)
    before = (lax.broadcasted_iota(jnp.int32, (blk, blk), 1)
              < lax.broadcasted_iota(jnp.int32, (blk, blk), 0)).astype(BF16)
    pos = loffrow_ref[0][0:1, :] + _dot(before, ind)
    p1 = jnp.sum(jnp.where(lane == e1, pos, 0.0), axis=1, keepdims=True).astype(jnp.int32)
    p2 = jnp.sum(jnp.where(lane == e2, pos, 0.0), axis=1, keepdims=True).astype(jnp.int32)
    rid = lax.broadcasted_iota(jnp.int32, (blk, rows), 1)
    pick = ((rid == p1) | (rid == p2)).astype(BF16)
    copies(wait=True)
    out = h_ref[...] + _dot(pick, buf[...])
    if final_norm:
        out = _rms(out, fg_ref[...])
    o_ref[...] = out


def _combine(h, meta, ys, plan, blk, rows, final_g, final_norm):
    tokens, d = h.shape
    nb = tokens // blk
    loffrow = jnp.pad(plan["loff_f"], ((0, 0), (0, LANES - N_EXPERTS)))
    loffrow = jnp.broadcast_to(loffrow[:, None, :], (nb, SUBLANES, LANES))
    grid_spec = pltpu.PrefetchScalarGridSpec(
        num_scalar_prefetch=3,
        grid=(nb,),
        in_specs=[pl.BlockSpec((blk, d), lambda b, *_: (b, 0)),
                  pl.BlockSpec((blk, LANES), lambda b, *_: (b, 0)),
                  pl.BlockSpec((1, SUBLANES, LANES), lambda b, *_: (b, 0, 0)),
                  pl.BlockSpec((1, d), lambda b, *_: (0, 0)),
                  pl.BlockSpec(memory_space=pl.ANY)],
        out_specs=pl.BlockSpec((blk, d), lambda b, *_: (b, 0)),
        scratch_shapes=[pltpu.VMEM((rows, d), BF16), pltpu.SemaphoreType.DMA((1,))],
    )
    return pl.pallas_call(
        functools.partial(_combine_kernel, blk=blk, rows=rows, final_norm=final_norm),
        grid_spec=grid_spec,
        out_shape=jax.ShapeDtypeStruct((tokens, d), F32),
        compiler_params=_params(("arbitrary",)),
        name="moe_combine",
    )(plan["granules"], plan["loff"], plan["gstart"], h, meta, loffrow, final_g.reshape(1, -1), ys)


def _moe_ffn(h, g, w_router, b_router, w_gate, w_up, w_down, final_g, final_norm):
    tokens, _ = h.shape
    blk = min(MOE_BLOCK, tokens)
    nb = tokens // blk
    tile = MOE_TILE
    rows = 2 * blk + N_EXPERTS * BF16_TILE_ROWS
    max_rows = 2 * tokens + nb * N_EXPERTS * (BF16_TILE_ROWS - 1) + N_EXPERTS * (tile - 1)
    n_tiles = -(-max_rows // tile)
    n_bf16, meta, meta_t, counts = _router(h, g, w_router, b_router)
    plan = _route_plan(counts, tile, n_tiles)
    xs = _gather(n_bf16, meta, meta_t, plan, blk, rows, n_tiles * tile)
    ys = _experts(xs, plan, w_gate.astype(BF16), w_up.astype(BF16), w_down.astype(BF16), tile)
    return _combine(h, meta, ys, plan, blk, rows, final_g, final_norm)


def _final_norm_kernel(h_ref, g_ref, o_ref):
    o_ref[...] = _rms(h_ref[...], g_ref[...])


def _final_norm(h, g):
    tokens, d = h.shape
    tm = min(FFN_TOKENS, tokens)
    tile = pl.BlockSpec((tm, d), lambda i: (i, 0))
    return pl.pallas_call(
        _final_norm_kernel, grid=(tokens // tm,), in_specs=[tile, _const_spec((1, d))], out_specs=tile,
        out_shape=jax.ShapeDtypeStruct((tokens, d), F32), compiler_params=_params(("arbitrary",)),
        name="final_norm",
    )(h, g.reshape(1, -1))


def kernel(x, norm_mix_g, norm_ffn_g, norm_final_g, conv_w_in, conv_b_in, conv_w_dw, conv_b_dw, conv_ln_g, conv_ln_b, conv_w_out, conv_b_out, pool_w_in, pool_w_grp, pool_b_grp, pool_scale, pool_w_out, sgu_w_in, sgu_b_in, sgu_ln_g, sgu_ln_b, sgu_w_s, sgu_b_s, sgu_w_out, ffn_w_gate, ffn_w_up, ffn_w_down, moe_w_router, moe_b_router, moe_w_gate, moe_w_up, moe_w_down):
    batch, seq, d = x.shape
    depth = norm_mix_g.shape[0]
    h = x.reshape(batch * seq, d)
    for i in range(depth):
        m, j = i % 3, i // 3
        if m == 0:
            h = _conv_mixer(h, seq, norm_mix_g[i], conv_w_in[j], conv_b_in[j], conv_w_dw[j], conv_b_dw[j],
                            conv_ln_g[j], conv_ln_b[j], conv_w_out[j], conv_b_out[j])
        elif m == 1:
            h = _pool_mixer(h, seq, norm_mix_g[i], pool_w_in[j], pool_w_grp[j], pool_b_grp[j],
                            pool_scale[j], pool_w_out[j])
        else:
            h = _sgu_mixer(h, seq, norm_mix_g[i], sgu_w_in[j], sgu_b_in[j], sgu_ln_g[j], sgu_ln_b[j],
                           sgu_w_s[j], sgu_b_s[j], sgu_w_out[j])
        c = i // 2
        last = i == depth - 1
        if i % 2 == 0:
            h = _dense_ffn(h, norm_ffn_g[i], ffn_w_gate[c], ffn_w_up[c], ffn_w_down[c])
            if last:
                h = _final_norm(h, norm_final_g)
        else:
            h = _moe_ffn(h, norm_ffn_g[i], moe_w_router[c], moe_b_router[c], moe_w_gate[c], moe_w_up[c],
                         moe_w_down[c], norm_final_g, last)
    return h.reshape(batch, seq, d)
```

```python
import collections
import functools

import jax
import jax.numpy as jnp
from jax import lax
from jax.experimental import pallas as pl
from jax.experimental.pallas import tpu as pltpu

F32 = jnp.float32
BF16 = jnp.bfloat16

RMS_EPS = 1e-6
LN_EPS = 1e-5
CONV_WIDTH = 31
POOL_WINDOWS = (2, 4, 8, 16)
SGU_GROUPS = 4
SGU_CHUNK = 128
N_EXPERTS = 8

LANES = 128
SUBLANES = 8
BF16_TILE_ROWS = 16
MXU_WIDTH = 256
VMEM_LIMIT_BYTES = 56 * 2**20

MIX_TOKENS = 512
MOE_BLOCK = 512
MOE_TILE = 1024
MOE_HIDDEN_STEPS = 2
GATE_PIECES = 3
ZERO_ROWS = 256
CONV_HALO = 32
POOL_HALO = 16
CONV_ROWS = 64


def _params(semantics):
    return pltpu.CompilerParams(dimension_semantics=semantics, vmem_limit_bytes=VMEM_LIMIT_BYTES)


def _rms(x, g):
    ms = jnp.mean(x * x, axis=-1, keepdims=True)
    return x * lax.rsqrt(ms + RMS_EPS) * g


def _layer_norm(x, g, b):
    mu = jnp.mean(x, axis=-1, keepdims=True)
    xc = x - mu
    var = jnp.mean(xc * xc, axis=-1, keepdims=True)
    return xc * lax.rsqrt(var + LN_EPS) * g + b


def _dot(a, b):
    return jnp.dot(a, b, preferred_element_type=F32)


def _row(v):
    return v.reshape(1, -1)


_Stage = collections.namedtuple("_Stage", ["name", "body", "arrays", "scratch"])
_Loop = collections.namedtuple("_Loop", ["trips", "body", "vector_bound"])


def _const_spec(shape):
    nd = len(shape)
    return pl.BlockSpec(shape, lambda *_: (0,) * nd, pipeline_mode=pl.Buffered(1))


def _conv_body(load_x, j, refs, scratch):
    g_ref, win_ref, bin_ref, wdw_ref, bdw_ref, lng_ref, lnb_ref, wout_ref, bout_ref = refs
    zbuf, acc, ybuf = scratch
    nblk, ts, _ = acc.shape

    @pl.when(j == 0)
    def _():
        zbuf[:, 0:CONV_HALO, :] = jnp.zeros((nblk, CONV_HALO, LANES), F32)

    n = _rms(load_x(), g_ref[...]).astype(BF16)
    y = _dot(n, win_ref[...]) + bin_ref[...]
    for b in range(2 * nblk):
        ybuf[b] = y[:, b * LANES:(b + 1) * LANES]
    yield

    first = CONV_HALO - (CONV_WIDTH - 1)

    def lane_block(jb):
        zbuf[jb, CONV_HALO:CONV_HALO + ts, :] = ybuf[jb] * jax.nn.sigmoid(ybuf[nblk + jb])
        for i in range(ts // CONV_ROWS):
            base = i * CONV_ROWS
            a = jnp.broadcast_to(bdw_ref[jb], (CONV_ROWS, LANES))
            for k in range(CONV_WIDTH):
                a = a + wdw_ref[jb, k:k + 1, :] * zbuf[jb, base + first + k:base + first + k + CONV_ROWS, :]
            acc[jb, base:base + CONV_ROWS, :] = a
        zbuf[jb, 0:CONV_HALO, :] = zbuf[jb, ts:ts + CONV_HALO, :]

    yield _Loop(nblk, lane_block, vector_bound=True)
    conv = jnp.concatenate([acc[b] for b in range(nblk)], axis=1)
    c = _layer_norm(conv, lng_ref[...], lnb_ref[...])
    s = (c * jax.nn.sigmoid(c)).astype(BF16)
    yield
    return _dot(s, wout_ref[...]) + bout_ref[...] + load_x()


def _lane_blocks(v):
    rows, ch = v.shape
    return v.reshape(rows, ch // LANES, LANES).transpose(1, 0, 2)


def _conv_stage(ts, g, w_in, b_in, w_dw, b_dw, ln_g, ln_b, w_out, b_out):
    nblk = w_out.shape[0] // LANES
    w_dw = jnp.pad(w_dw, ((0, CONV_HALO - CONV_WIDTH), (0, 0)))
    return _Stage(
        "conv", _conv_body,
        [_row(g), w_in.astype(BF16), _row(b_in), _lane_blocks(w_dw), _lane_blocks(_row(b_dw)), _row(ln_g),
         _row(ln_b), w_out.astype(BF16), _row(b_out)],
        [pltpu.VMEM((nblk, ts + CONV_HALO, LANES), F32), pltpu.VMEM((nblk, ts, LANES), F32),
         pltpu.VMEM((2 * nblk, ts, LANES), F32)])


def _pool_body(load_x, j, refs, scratch):
    g_ref, win_ref, wgrp_ref, bgrp_ref, scale_ref, wout_ref = refs
    (pbuf,) = scratch
    ch = pbuf.shape[1]
    ts = pbuf.shape[0] - POOL_HALO

    @pl.when(j == 0)
    def _():
        pbuf[0:POOL_HALO, :] = jnp.zeros((POOL_HALO, ch), F32)

    n = _rms(load_x(), g_ref[...]).astype(BF16)
    pbuf[POOL_HALO:POOL_HALO + ts, :] = _dot(n, win_ref[...])
    yield
    t = j * ts + lax.broadcasted_iota(jnp.int32, (ts, 1), 0)
    cg = ch // len(POOL_WINDOWS)
    outs = []
    for g, w in enumerate(POOL_WINDOWS):
        cols = slice(g * cg, (g + 1) * cg)
        pg = pbuf[POOL_HALO:POOL_HALO + ts, cols]
        s = pg
        for back in range(1, w):
            s = s + pbuf[POOL_HALO - back:POOL_HALO - back + ts, cols]
        cnt = jnp.minimum(t + 1, w).astype(F32)
        zg = (s / cnt - pg).astype(BF16)
        outs.append(_dot(zg, wgrp_ref[g]) + bgrp_ref[g:g + 1, :])
        yield
    z = (jnp.concatenate(outs, axis=1) * scale_ref[...]).astype(BF16)
    pbuf[0:POOL_HALO, :] = pbuf[ts:ts + POOL_HALO, :]
    return _dot(z, wout_ref[...]) + load_x()


def _pool_stage(ts, g, w_in, w_grp, b_grp, scale, w_out):
    ch = w_in.shape[1]
    return _Stage(
        "pool", _pool_body,
        [_row(g), w_in.astype(BF16), w_grp.astype(BF16), b_grp, _row(scale), w_out.astype(BF16)],
        [pltpu.VMEM((ts + POOL_HALO, ch), F32)])


def _sgu_body(load_x, j, refs, scratch):
    del j
    g_ref, win_ref, bin_ref, lng_ref, lnb_ref, ws_ref, bs_ref, wout_ref = refs
    (ybuf,) = scratch
    nchunk, ts, _ = ybuf.shape
    ch = wout_ref.shape[0]
    n = _rms(load_x(), g_ref[...]).astype(BF16)
    y = _dot(n, win_ref[...]) + bin_ref[...]
    for c in range(nchunk):
        ybuf[c] = y[:, c * MXU_WIDTH:(c + 1) * MXU_WIDTH]
    yield

    def gelu_chunk(c):
        t = ybuf[c]
        ybuf[c] = 0.5 * t * (1.0 + lax.erf(t * (2.0 ** -0.5)))

    yield _Loop(nchunk, gelu_chunk, vector_bound=True)
    half = nchunk // 2
    v = jnp.concatenate([ybuf[c] for c in range(half, nchunk)], axis=1)
    v = _layer_norm(v, lng_ref[...], lnb_ref[...]).astype(BF16)
    yield
    u = jnp.concatenate([ybuf[c] for c in range(half)], axis=1)
    cg = ch // SGU_GROUPS
    causal = (lax.broadcasted_iota(jnp.int32, (SGU_CHUNK, SGU_CHUNK), 0)
              >= lax.broadcasted_iota(jnp.int32, (SGU_CHUNK, SGU_CHUNK), 1))
    groups = []
    for g in range(SGU_GROUPS):
        ws = jnp.where(causal, ws_ref[g], 0.0).astype(BF16)
        chunks = [_dot(ws, v[c * SGU_CHUNK:(c + 1) * SGU_CHUNK, g * cg:(g + 1) * cg]) + bs_ref[g]
                  for c in range(ts // SGU_CHUNK)]
        groups.append(jnp.concatenate(chunks, axis=0))
    mixed = jnp.concatenate(groups, axis=1)
    return _dot((u * mixed).astype(BF16), wout_ref[...]) + load_x()


def _sgu_stage(ts, g, w_in, b_in, ln_g, ln_b, w_s, b_s, w_out):
    return _Stage(
        "sgu", _sgu_body,
        [_row(g), w_in.astype(BF16), _row(b_in), _row(ln_g), _row(ln_b), w_s, b_s[:, :, None],
         w_out.astype(BF16)],
        [pltpu.VMEM((w_in.shape[1] // MXU_WIDTH, ts, MXU_WIDTH), F32)])


def _swiglu(x, wg, wu):
    a = _dot(x, wg)
    return (a * jax.nn.sigmoid(a) * _dot(x, wu)).astype(BF16)


def _ffn_body(load_x, j, refs, scratch):
    del j
    g_ref, wg_ref, wu_ref, wd_ref = refs
    nbuf, hbuf = scratch
    nbuf[...] = _rms(load_x(), g_ref[...]).astype(BF16)
    yield

    def chunk(c):
        hbuf[c] = _swiglu(nbuf[...], wg_ref[c], wu_ref[c])

    nchunk = hbuf.shape[0]
    yield _Loop(nchunk, chunk, vector_bound=False)
    hidden = jnp.concatenate([hbuf[c] for c in range(nchunk)], axis=1)
    return load_x() + _dot(hidden, wd_ref[...])


def _column_chunks(w):
    d, n = w.shape
    return w.reshape(d, n // MXU_WIDTH, MXU_WIDTH).transpose(1, 0, 2)


def _ffn_stage(ts, g, w_gate, w_up, w_down):
    d, hidden = w_gate.shape
    return _Stage(
        "ffn", _ffn_body,
        [_row(g), _column_chunks(w_gate.astype(BF16)), _column_chunks(w_up.astype(BF16)),
         w_down.astype(BF16)],
        [pltpu.VMEM((ts, d), BF16), pltpu.VMEM((hidden // MXU_WIDTH, ts, MXU_WIDTH), BF16)])


def _scheduling_block(step, units):
    def body(_, carry):
        for unit in units:
            unit()
        return carry

    lax.fori_loop(0, jnp.where(step >= 0, 1, 0), body, 0)


def _chain_kernel(*refs, stages, n_tiles, steps):
    depth = len(stages)
    h_ref = refs[0]
    pos = 1
    operands = []
    for st in stages:
        operands.append(refs[pos:pos + len(st.arrays)])
        pos += len(st.arrays)
    o_ref = refs[pos]
    handoff = refs[pos + 1:pos + depth]
    pos += depth
    scratch = []
    for st in stages:
        scratch.append(refs[pos:pos + len(st.scratch)])
        pos += len(st.scratch)

    s = pl.program_id(0)

    @pl.when(s == 0)
    def _():
        for buf in handoff:
            buf[...] = jnp.zeros(buf.shape, F32)

    order = list(reversed(range(depth)))
    slots = {k: (s + 2 * depth - k) & 1 for k in order}

    def loader(k):
        if k == 0:
            return lambda: h_ref[...]
        return lambda: handoff[k - 1][slots[k]]

    running = {}
    for k in order:
        tile = jnp.clip(s - k, 0, n_tiles - 1)
        running[k] = stages[k].body(loader(k), lax.rem(tile, steps), operands[k], scratch[k])
    while running:
        loops = []
        for k in order:
            if k not in running:
                continue
            try:
                unit = next(running[k])
            except StopIteration as done:
                del running[k]
                if k == depth - 1:
                    o_ref[...] = done.value
                else:
                    handoff[k][slots[k]] = done.value
                continue
            if unit is not None:
                loops.append(unit)
        loops.sort(key=lambda lp: not lp.vector_bound)
        for i in range(max((lp.trips for lp in loops), default=0)):
            units = [functools.partial(lp.body, i) for lp in loops if i < lp.trips]
            if len(loops) > 1:
                _scheduling_block(s, units)
            else:
                units[0]()


def _chain(h, seq, ts, stages):
    tokens, d = h.shape
    n_tiles = tokens // ts
    depth = len(stages)
    in_specs = [pl.BlockSpec((ts, d), lambda s: (jnp.minimum(s, n_tiles - 1), 0))]
    arrays = []
    for st in stages:
        arrays += st.arrays
        in_specs += [_const_spec(a.shape) for a in st.arrays]
    scratch = [pltpu.VMEM((2, ts, d), F32) for _ in range(depth - 1)]
    for st in stages:
        scratch += st.scratch
    return pl.pallas_call(
        functools.partial(_chain_kernel, stages=stages, n_tiles=n_tiles, steps=seq // ts),
        grid=(n_tiles + depth - 1,),
        in_specs=in_specs,
        out_specs=pl.BlockSpec((ts, d), lambda s: (jnp.maximum(s - (depth - 1), 0), 0)),
        out_shape=jax.ShapeDtypeStruct((tokens, d), F32),
        scratch_shapes=scratch,
        compiler_params=_params(("arbitrary",)),
        name="_".join(st.name for st in stages),
    )(h, *arrays)


def _router_kernel(h_ref, g_ref, wr_ref, br_ref, n_ref, meta_ref, meta_t_ref, cnt_ref):
    blk = h_ref.shape[0]
    n = _rms(h_ref[...], g_ref[...])
    wr = wr_ref[...]
    n_hi, w_hi = n.astype(BF16), wr.astype(BF16)
    n_ref[...] = n_hi
    n_lo = (n - n_hi.astype(F32)).astype(BF16)
    w_lo = (wr - w_hi.astype(F32)).astype(BF16)
    logits = _dot(n_hi, w_hi) + (_dot(n_lo, w_hi) + _dot(n_hi, w_lo)) + br_ref[...]
    lane = lax.broadcasted_iota(jnp.int32, (blk, LANES), 1)
    neg = jnp.float32(-1e30)
    lg = jnp.where(lane < N_EXPERTS, logits, neg)
    m1 = jnp.max(lg, axis=1, keepdims=True)
    i1 = jnp.min(jnp.where(lg == m1, lane, LANES), axis=1, keepdims=True)
    lg2 = jnp.where(lane == i1, neg, lg)
    m2 = jnp.max(lg2, axis=1, keepdims=True)
    i2 = jnp.min(jnp.where(lg2 == m2, lane, LANES), axis=1, keepdims=True)
    e = jnp.exp(m2 - m1)
    g1 = 1.0 / (1.0 + e)
    g2 = e / (1.0 + e)
    meta = jnp.where(lane == 0, i1.astype(F32),
                     jnp.where(lane == 1, i2.astype(F32),
                               jnp.where(lane == 2, g1, jnp.where(lane == 3, g2, 0.0))))
    meta_ref[...] = meta
    meta_t_ref[...] = meta.T[0:SUBLANES, :]
    onehot = ((lane == i1) | (lane == i2)).astype(F32)
    cnt_ref[0] = jnp.broadcast_to(jnp.sum(onehot, axis=0, keepdims=True), (SUBLANES, LANES))


def _router(h, g, w_router, b_router):
    tokens, d = h.shape
    blk = min(MOE_BLOCK, tokens)
    nb = tokens // blk
    wr = jnp.pad(w_router, ((0, 0), (0, LANES - N_EXPERTS)))
    br = jnp.pad(b_router, (0, LANES - N_EXPERTS)).reshape(1, LANES)
    return pl.pallas_call(
        _router_kernel,
        grid=(nb,),
        in_specs=[pl.BlockSpec((blk, d), lambda b: (b, 0)), _const_spec((1, d)),
                  _const_spec((d, LANES)), _const_spec((1, LANES))],
        out_specs=[pl.BlockSpec((blk, d), lambda b: (b, 0)),
                   pl.BlockSpec((blk, LANES), lambda b: (b, 0)),
                   pl.BlockSpec((SUBLANES, blk), lambda b: (0, b)),
                   pl.BlockSpec((1, SUBLANES, LANES), lambda b: (b, 0, 0))],
        out_shape=[jax.ShapeDtypeStruct((tokens, d), BF16),
                   jax.ShapeDtypeStruct((tokens, LANES), F32),
                   jax.ShapeDtypeStruct((SUBLANES, tokens), F32),
                   jax.ShapeDtypeStruct((nb, SUBLANES, LANES), F32)],
        compiler_params=_params(("arbitrary",)),
        name="moe_router",
    )(h, g.reshape(1, -1), wr, br)


def _route_plan(counts, tile, n_tiles):
    c = counts[:, 0, :N_EXPERTS].astype(jnp.int32)
    cpad = (c + BF16_TILE_ROWS - 1) // BF16_TILE_ROWS * BF16_TILE_ROWS
    loff = jnp.cumsum(cpad, axis=1) - cpad
    used = jnp.sum(cpad, axis=0)
    region = (used + tile - 1) // tile * tile
    ends = jnp.cumsum(region)
    gstart = (ends - region)[None, :] + jnp.cumsum(cpad, axis=0) - cpad
    n_valid = ends[-1] // tile
    tidx = jnp.minimum(jnp.arange(n_tiles, dtype=jnp.int32), jnp.maximum(n_valid - 1, 0))
    tile_expert = jnp.minimum(jnp.searchsorted(ends, tidx * tile, side="right"), N_EXPERTS - 1)
    return dict(
        granules=(cpad // BF16_TILE_ROWS).reshape(-1),
        loff=loff.reshape(-1),
        gstart=gstart.reshape(-1),
        tail_start=jnp.concatenate([ends - region + used, ends[-1:]]),
        tail_granules=jnp.concatenate([(region - used) // BF16_TILE_ROWS,
                                       (n_tiles * tile - ends[-1:]) // ZERO_ROWS]),
        loff_f=loff.astype(F32),
        tile_block=tidx.astype(jnp.int32),
        tile_expert=tile_expert.astype(jnp.int32),
        n_valid=n_valid.reshape(1).astype(jnp.int32),
    )


def _rows16(base, i):
    return pl.ds(pl.multiple_of(base + i * BF16_TILE_ROWS, BF16_TILE_ROWS), BF16_TILE_ROWS)


def _granule_loop(count, copy_at, wait):
    def body(i, carry):
        cp = copy_at(i)
        if wait:
            cp.wait()
        else:
            cp.start()
        return carry

    lax.fori_loop(0, count, body, 0)


def _block_copies(block, granules_ref, loff_ref, gstart_ref, local_ref, global_ref, sem, *,
                  to_global, wait):
    for e in range(N_EXPERTS):
        idx = block * N_EXPERTS + e
        lbase = loff_ref[idx]
        gbase = gstart_ref[idx]

        def copy_at(i, lbase=lbase, gbase=gbase):
            local = local_ref.at[_rows16(lbase, i), :]
            remote = global_ref.at[_rows16(gbase, i), :]
            if to_global:
                return pltpu.make_async_copy(local, remote, sem)
            return pltpu.make_async_copy(remote, local, sem)

        _granule_loop(granules_ref[idx], copy_at, wait)


def _tail_fill(tail_start_ref, tail_granules_ref, zrows, sorted_ref, sem, *, wait):
    for e in range(N_EXPERTS):
        base = tail_start_ref[e]

        def copy_at(i, base=base):
            return pltpu.make_async_copy(zrows.at[0:BF16_TILE_ROWS, :], sorted_ref.at[_rows16(base, i), :], sem)

        _granule_loop(tail_granules_ref[e], copy_at, wait)

    chunk = zrows.shape[0]
    end_base = tail_start_ref[N_EXPERTS]

    def end_copy_at(i):
        rows = pl.ds(pl.multiple_of(end_base + i * chunk, chunk), chunk)
        return pltpu.make_async_copy(zrows, sorted_ref.at[rows, :], sem)

    _granule_loop(tail_granules_ref[N_EXPERTS], end_copy_at, wait)


def _gate_pieces(g, lane):
    out = jnp.zeros(lane.shape, F32)
    rest = g
    for i in range(GATE_PIECES):
        piece = rest.astype(BF16).astype(F32)
        out = jnp.where(lane == i, piece, out)
        rest = rest - piece
    return out.astype(BF16)


def _gather_kernel(granules_ref, loff_ref, gstart_ref, tail_start_ref, tail_granules_ref,
                   n_ref, meta_ref, meta_t_ref, loffcol_ref, sorted_ref, xs, zrows, sem, *, blk, rows, d):
    b = pl.program_id(0)
    e1 = meta_t_ref[0:1, :]
    e2 = meta_t_ref[1:2, :]
    eid = lax.broadcasted_iota(jnp.int32, (SUBLANES, blk), 0).astype(F32)
    ind = ((eid == e1) | (eid == e2)).astype(BF16)
    before = (lax.broadcasted_iota(jnp.int32, (blk, blk), 0)
              < lax.broadcasted_iota(jnp.int32, (blk, blk), 1)).astype(BF16)
    pos = loffcol_ref[0][:, 0:1] + _dot(ind, before)
    p1 = jnp.sum(jnp.where(eid == e1, pos, 0.0), axis=0, keepdims=True).astype(jnp.int32)
    p2 = jnp.sum(jnp.where(eid == e2, pos, 0.0), axis=0, keepdims=True).astype(jnp.int32)
    rid = lax.broadcasted_iota(jnp.int32, (rows, blk), 0)
    perm1 = (rid == p1).astype(BF16)
    perm2 = (rid == p2).astype(BF16)
    xs[:, 0:d] = _dot(perm1 + perm2, n_ref[...]).astype(BF16)
    meta = meta_ref[...]
    lane = lax.broadcasted_iota(jnp.int32, (blk, LANES), 1)
    gates = _dot(perm1, _gate_pieces(meta[:, 2:3], lane)) + _dot(perm2, _gate_pieces(meta[:, 3:4], lane))
    xs[:, d:d + LANES] = gates.astype(BF16)
    copies = functools.partial(_block_copies, b, granules_ref, loff_ref, gstart_ref, xs, sorted_ref,
                               sem.at[0], to_global=True)
    copies(wait=False)
    copies(wait=True)

    @pl.when(b == pl.num_programs(0) - 1)
    def _():
        zrows[...] = jnp.zeros(zrows.shape, BF16)
        fill = functools.partial(_tail_fill, tail_start_ref, tail_granules_ref, zrows, sorted_ref, sem.at[0])
        fill(wait=False)
        fill(wait=True)


def _gather(n_bf16, meta, meta_t, plan, blk, rows, total_rows):
    tokens, d = n_bf16.shape
    nb = tokens // blk
    loffcol = jnp.broadcast_to(plan["loff_f"][:, :, None], (nb, N_EXPERTS, LANES))
    width = d + LANES
    grid_spec = pltpu.PrefetchScalarGridSpec(
        num_scalar_prefetch=5,
        grid=(nb,),
        in_specs=[pl.BlockSpec((blk, d), lambda b, *_: (b, 0)),
                  pl.BlockSpec((blk, LANES), lambda b, *_: (b, 0)),
                  pl.BlockSpec((SUBLANES, blk), lambda b, *_: (0, b)),
                  pl.BlockSpec((1, N_EXPERTS, LANES), lambda b, *_: (b, 0, 0))],
        out_specs=pl.BlockSpec(memory_space=pl.ANY),
        scratch_shapes=[pltpu.VMEM((rows, width), BF16), pltpu.VMEM((ZERO_ROWS, width), BF16),
                        pltpu.SemaphoreType.DMA((1,))],
    )
    return pl.pallas_call(
        functools.partial(_gather_kernel, blk=blk, rows=rows, d=d),
        grid_spec=grid_spec,
        out_shape=jax.ShapeDtypeStruct((total_rows, width), BF16),
        compiler_params=_params(("arbitrary",)),
        name="moe_gather",
    )(plan["granules"], plan["loff"], plan["gstart"], plan["tail_start"], plan["tail_granules"],
      n_bf16, meta, meta_t, loffcol)


def _expert_kernel(tile_block_ref, tile_expert_ref, n_valid_ref, x_ref, wg_ref, wu_ref, wd_ref, o_ref,
                   acc, hbuf, *, d):
    del tile_block_ref, tile_expert_ref
    j = pl.program_id(1)
    used = pl.program_id(0) < n_valid_ref[0]

    @pl.when(jnp.logical_not(used) & (j == 0))
    def _():
        o_ref[...] = jnp.zeros(o_ref.shape, BF16)

    @pl.when((pl.program_id(0) == 0) & (j == 0))
    def _():
        acc[...] = jnp.zeros(acc.shape, F32)

    @pl.when(used)
    def _():
        x = x_ref[:, 0:d]
        for c in range(hbuf.shape[1] // MXU_WIDTH):
            cols = slice(c * MXU_WIDTH, (c + 1) * MXU_WIDTH)
            hbuf[:, cols] = _swiglu(x, wg_ref[:, cols], wu_ref[:, cols])
        down = _dot(hbuf[...], wd_ref[...])
        total = jnp.where(j == 0, down, acc[...] + down)
        acc[...] = total
        pieces = x_ref[:, d:d + LANES].astype(F32)
        gate = sum(pieces[:, i:i + 1] for i in range(1, GATE_PIECES)) + pieces[:, 0:1]
        o_ref[...] = (total * gate).astype(BF16)


def _experts(xs, plan, layer, w_gate, w_up, w_down, tile):
    total_rows = xs.shape[0]
    d = w_gate.shape[2]
    hidden = w_gate.shape[3]
    steps = MOE_HIDDEN_STEPS
    th = hidden // steps

    def hid(i, j, nv):
        return jnp.where(i < nv[0], j, steps - 1)

    grid_spec = pltpu.PrefetchScalarGridSpec(
        num_scalar_prefetch=3,
        grid=(total_rows // tile, steps),
        in_specs=[pl.BlockSpec((tile, d + LANES), lambda i, j, tb, te, nv: (tb[i], 0)),
                  pl.BlockSpec((None, None, d, th), lambda i, j, tb, te, nv: (layer, te[i], 0, hid(i, j, nv))),
                  pl.BlockSpec((None, None, d, th), lambda i, j, tb, te, nv: (layer, te[i], 0, hid(i, j, nv))),
                  pl.BlockSpec((None, None, th, d), lambda i, j, tb, te, nv: (layer, te[i], hid(i, j, nv), 0))],
        out_specs=pl.BlockSpec((tile, d), lambda i, j, tb, te, nv: (i, 0)),
        scratch_shapes=[pltpu.VMEM((tile, d), F32), pltpu.VMEM((tile, th), BF16)],
    )
    return pl.pallas_call(
        functools.partial(_expert_kernel, d=d),
        grid_spec=grid_spec,
        out_shape=jax.ShapeDtypeStruct((total_rows, d), BF16),
        compiler_params=_params(("arbitrary", "arbitrary")),
        name="moe_experts",
    )(plan["tile_block"], plan["tile_expert"], plan["n_valid"], xs, w_gate, w_up, w_down)


def _combine_kernel(granules_ref, loff_ref, gstart_ref, h_ref, meta_ref, loffrow_ref, fg_ref, ys_ref,
                    o_ref, buf, sem, *, blk, rows, final_norm):
    b = pl.program_id(0)

    @pl.when(b == 0)
    def _():
        buf[...] = jnp.zeros(buf.shape, BF16)

    copies = functools.partial(_block_copies, b, granules_ref, loff_ref, gstart_ref, buf, ys_ref,
                               sem.at[0], to_global=False)
    copies(wait=False)
    meta = meta_ref[...]
    e1, e2 = meta[:, 0:1], meta[:, 1:2]
    lane = lax.broadcasted_iota(jnp.int32, (blk, LANES), 1).astype(F32)
    ind = ((lane == e1) | (lane == e2)).astype(BF16)
    before = (lax.broadcasted_iota(jnp.int32, (blk, blk), 1)
              < lax.broadcasted_iota(jnp.int32, (blk, blk), 0)).astype(BF16)
    pos = loffrow_ref[0][0:1, :] + _dot(before, ind)
    p1 = jnp.sum(jnp.where(lane == e1, pos, 0.0), axis=1, keepdims=True).astype(jnp.int32)
    p2 = jnp.sum(jnp.where(lane == e2, pos, 0.0), axis=1, keepdims=True).astype(jnp.int32)
    rid = lax.broadcasted_iota(jnp.int32, (blk, rows), 1)
    pick = ((rid == p1) | (rid == p2)).astype(BF16)
    copies(wait=True)
    out = h_ref[...] + _dot(pick, buf[...])
    if final_norm:
        out = _rms(out, fg_ref[...])
    o_ref[...] = out


def _combine(h, meta, ys, plan, blk, rows, final_g, final_norm):
    tokens, d = h.shape
    nb = tokens // blk
    loffrow = jnp.pad(plan["loff_f"], ((0, 0), (0, LANES - N_EXPERTS)))
    loffrow = jnp.broadcast_to(loffrow[:, None, :], (nb, SUBLANES, LANES))
    grid_spec = pltpu.PrefetchScalarGridSpec(
        num_scalar_prefetch=3,
        grid=(nb,),
        in_specs=[pl.BlockSpec((blk, d), lambda b, *_: (b, 0)),
                  pl.BlockSpec((blk, LANES), lambda b, *_: (b, 0)),
                  pl.BlockSpec((1, SUBLANES, LANES), lambda b, *_: (b, 0, 0)),
                  pl.BlockSpec((1, d), lambda b, *_: (0, 0)),
                  pl.BlockSpec(memory_space=pl.ANY)],
        out_specs=pl.BlockSpec((blk, d), lambda b, *_: (b, 0)),
        scratch_shapes=[pltpu.VMEM((rows, d), BF16), pltpu.SemaphoreType.DMA((1,))],
    )
    return pl.pallas_call(
        functools.partial(_combine_kernel, blk=blk, rows=rows, final_norm=final_norm),
        grid_spec=grid_spec,
        out_shape=jax.ShapeDtypeStruct((tokens, d), F32),
        compiler_params=_params(("arbitrary",)),
        name="moe_combine",
    )(plan["granules"], plan["loff"], plan["gstart"], h, meta, loffrow, final_g.reshape(1, -1), ys)


def _moe_ffn(h, g, w_router, b_router, layer, w_gate, w_up, w_down, final_g, final_norm):
    tokens, _ = h.shape
    blk = min(MOE_BLOCK, tokens)
    nb = tokens // blk
    tile = MOE_TILE
    rows = 2 * blk + N_EXPERTS * BF16_TILE_ROWS
    max_rows = 2 * tokens + nb * N_EXPERTS * (BF16_TILE_ROWS - 1) + N_EXPERTS * (tile - 1)
    n_tiles = -(-max_rows // tile)
    n_bf16, meta, meta_t, counts = _router(h, g, w_router, b_router)
    plan = _route_plan(counts, tile, n_tiles)
    xs = _gather(n_bf16, meta, meta_t, plan, blk, rows, n_tiles * tile)
    ys = _experts(xs, plan, layer, w_gate, w_up, w_down, tile)
    return _combine(h, meta, ys, plan, blk, rows, final_g, final_norm)


def _pair_up(stages):
    best = {"conv", "ffn"}
    chains, i = [], 0
    while i < len(stages):
        names = [st.name for st in stages[i:i + 3]]
        wait_for_next = len(names) == 3 and set(names[1:]) == best and set(names[:2]) != best
        width = 1 if wait_for_next or len(names) == 1 else 2
        chains.append(stages[i:i + width])
        i += width
    return chains


def _norm_body(load_x, j, refs, scratch):
    del j, scratch
    yield
    return _rms(load_x(), refs[0][...])


def _norm_stage(g):
    return _Stage("norm", _norm_body, [_row(g)], [])


def kernel(x, norm_mix_g, norm_ffn_g, norm_final_g, conv_w_in, conv_b_in, conv_w_dw, conv_b_dw, conv_ln_g, conv_ln_b, conv_w_out, conv_b_out, pool_w_in, pool_w_grp, pool_b_grp, pool_scale, pool_w_out, sgu_w_in, sgu_b_in, sgu_ln_g, sgu_ln_b, sgu_w_s, sgu_b_s, sgu_w_out, ffn_w_gate, ffn_w_up, ffn_w_down, moe_w_router, moe_b_router, moe_w_gate, moe_w_up, moe_w_down):
    batch, seq, d = x.shape
    depth = norm_mix_g.shape[0]
    ts = min(MIX_TOKENS, seq)
    h = x.reshape(batch * seq, d)
    moe_bf16 = (moe_w_gate.astype(BF16), moe_w_up.astype(BF16), moe_w_down.astype(BF16))
    pending = []

    def flush(h, pending):
        for stages in _pair_up(pending):
            h = _chain(h, seq, ts, stages)
        return h

    for i in range(depth):
        m, j = i % 3, i // 3
        if m == 0:
            pending.append(_conv_stage(ts, norm_mix_g[i], conv_w_in[j], conv_b_in[j], conv_w_dw[j],
                                       conv_b_dw[j], conv_ln_g[j], conv_ln_b[j], conv_w_out[j],
                                       conv_b_out[j]))
        elif m == 1:
            pending.append(_pool_stage(ts, norm_mix_g[i], pool_w_in[j], pool_w_grp[j], pool_b_grp[j],
                                       pool_scale[j], pool_w_out[j]))
        else:
            pending.append(_sgu_stage(ts, norm_mix_g[i], sgu_w_in[j], sgu_b_in[j], sgu_ln_g[j],
                                      sgu_ln_b[j], sgu_w_s[j], sgu_b_s[j], sgu_w_out[j]))
        c = i // 2
        last = i == depth - 1
        if i % 2 == 0:
            pending.append(_ffn_stage(ts, norm_ffn_g[i], ffn_w_gate[c], ffn_w_up[c], ffn_w_down[c]))
            if last:
                pending.append(_norm_stage(norm_final_g))
        else:
            h = flush(h, pending)
            pending = []
            h = _moe_ffn(h, norm_ffn_g[i], moe_w_router[c], moe_b_router[c], c, *moe_bf16, norm_final_g, last)
    h = flush(h, pending)
    return h.reshape(batch, seq, d)
```
